```python
import math
import jax, jax.numpy as jnp
from jax import lax
import numpy as np

D_MODEL = 4096
BATCH = 2
SEQ = 8192
DEPTH = 2

HEAD_DIM = 128
BLOCK = 128
NORM_EPS = 1e-6
DIL_GROUPS = ((128, 1), (512, 4), (2048, 16))
A_HEADS_PER_GROUP = 4
A_HEADS = A_HEADS_PER_GROUP * len(DIL_GROUPS)
A_OUT = A_HEADS_PER_GROUP * HEAD_DIM
B_HEADS = 8
B_DK = 64
B_DV = 128
B_GATE_RANK = 16
B_GATE_TAU = 16.0
B_CHUNK = 64
B_OUT = B_HEADS * B_DV
C_Q_HEADS = 12
C_KV_HEADS = 3
C_WINDOW = 128
C_OUT = C_Q_HEADS * HEAD_DIM
REL_BUCKETS = 32
REL_MAX_DIST = 2048
REL_HEADS = A_HEADS + C_Q_HEADS
D_FF = 11008
CONV_W = 3
IN_WIDTHS = (A_HEADS * HEAD_DIM, A_HEADS * HEAD_DIM, A_HEADS * HEAD_DIM,
             B_HEADS * B_DK, B_HEADS * B_DK, B_HEADS * B_DV, B_HEADS * B_DV, B_GATE_RANK,
             C_Q_HEADS * HEAD_DIM, C_KV_HEADS * HEAD_DIM, C_KV_HEADS * HEAD_DIM,
             D_MODEL, D_MODEL, D_MODEL)
N_IN = sum(IN_WIDTHS)

kernel_name = "hybrid_gated_dilated_gla_swa_convffn"


def _rmsnorm(x, g):
    x32 = x.astype(jnp.float32)
    y = x32 * lax.rsqrt(jnp.mean(x32 * x32, axis=-1, keepdims=True) + NORM_EPS)
    return (y * g.astype(jnp.float32)).astype(x.dtype)


def _t5_bucket(dist):
    dist = jnp.maximum(dist, 0)
    max_exact = REL_BUCKETS // 2
    far = max_exact + (jnp.log(jnp.maximum(dist, 1).astype(jnp.float32) / max_exact)
                       / math.log(REL_MAX_DIST / max_exact)
                       * (REL_BUCKETS - max_exact)).astype(jnp.int32)
    return jnp.where(dist < max_exact, dist, jnp.minimum(far, REL_BUCKETS - 1))


def _band_mask(nb, max_steps):
    qi = jnp.arange(BLOCK)[:, None]
    kj = jnp.arange(2 * BLOCK)[None, :]
    rel = qi + BLOCK - kj
    band = (rel >= 0) & (rel <= max_steps)
    valid = (jnp.arange(nb)[:, None, None] > 0) | (kj >= BLOCK)[None]
    return rel, band[None] & valid


def _with_prev_block(t, axis):
    pad = [(0, 0)] * t.ndim
    pad[axis] = (1, 0)
    prev = lax.slice_in_dim(jnp.pad(t, pad), 0, t.shape[axis], axis=axis)
    return jnp.concatenate([prev, t], axis=axis + 1)


def _dilated_group(q, k, v, bias_tab, window, dilation):
    b, s, h, e = q.shape
    span = window // dilation
    L = s // dilation
    nb = -(-L // BLOCK)
    Lp = nb * BLOCK

    def to_stream(t):
        t = t.reshape(b, L, dilation, h, e).transpose(0, 2, 1, 3, 4)
        t = jnp.pad(t, ((0, 0), (0, 0), (0, Lp - L), (0, 0), (0, 0)))
        return t.reshape(b, dilation, nb, BLOCK, h, e)

    qs = to_stream(q)
    kk = _with_prev_block(to_stream(k), 2)
    vv = _with_prev_block(to_stream(v), 2)
    rel, mask = _band_mask(nb, span)
    bias = bias_tab.astype(jnp.float32)[_t5_bucket(rel * dilation)].transpose(2, 0, 1)
    logits = jnp.einsum('bdnqhe,bdnkhe->bdnhqk', qs, kk,
                        preferred_element_type=jnp.float32) * (e ** -0.5) + bias
    logits = jnp.where(mask[:, None], logits, -jnp.inf)
    lse = jax.nn.logsumexp(logits, axis=-1)
    p = jnp.exp(logits - lse[..., None])
    o = jnp.einsum('bdnhqk,bdnkhe->bdnqhe', p.astype(v.dtype), vv)
    o = o.reshape(b, dilation, Lp, h, e)[:, :, :L].transpose(0, 2, 1, 3, 4).reshape(b, s, h, e)
    lse = lse.transpose(0, 1, 2, 4, 3).reshape(b, dilation, Lp, h)[:, :, :L]
    lse = lse.transpose(0, 2, 1, 3).reshape(b, s, h)
    return o, lse


def _swa_sinks(q, k, v, bias_tab, sinks):
    b, s, hq, e = q.shape
    hkv = k.shape[2]
    g = hq // hkv
    nb = s // BLOCK
    qb = q.reshape(b, nb, BLOCK, hkv, g, e)
    kk = _with_prev_block(k.reshape(b, nb, BLOCK, hkv, e), 1)
    vv = _with_prev_block(v.reshape(b, nb, BLOCK, hkv, e), 1)
    rel, mask = _band_mask(nb, C_WINDOW - 1)
    bias = bias_tab.astype(jnp.float32)[_t5_bucket(rel)].transpose(2, 0, 1).reshape(hkv, g, BLOCK, 2 * BLOCK)
    logits = jnp.einsum('bnqhge,bnkhe->bnhgqk', qb, kk,
                        preferred_element_type=jnp.float32) * (e ** -0.5) + bias
    logits = jnp.where(mask[:, None, None], logits, -jnp.inf)
    sink = sinks.astype(jnp.float32).reshape(hkv, g, 1, 1)
    m = jnp.maximum(jnp.max(logits, axis=-1, keepdims=True), sink)
    w = jnp.exp(logits - m)
    p = w / (jnp.sum(w, axis=-1, keepdims=True) + jnp.exp(sink - m))
    o = jnp.einsum('bnhgqk,bnkhe->bnqhge', p.astype(v.dtype), vv)
    return o.reshape(b, s, hq * e)


def _gla(q, k, v, log_a):
    b, s, h, dk = q.shape
    dv = v.shape[-1]
    nc = s // B_CHUNK
    q = q.reshape(b, nc, B_CHUNK, h, dk) * (dk ** -0.5)
    k = k.reshape(b, nc, B_CHUNK, h, dk)
    v = v.reshape(b, nc, B_CHUNK, h, dv)
    cum = lax.cumsum(log_a.reshape(b, nc, B_CHUNK, h, dk), axis=2)
    last = cum[:, :, -1:]
    q_dec = q * jnp.exp(cum)
    k_inv = k * jnp.exp(-cum)
    k_out = k * jnp.exp(last - cum)
    causal = jnp.tril(jnp.ones((B_CHUNK, B_CHUNK), dtype=bool))
    att = jnp.where(causal, jnp.einsum('bnihk,bnjhk->bnhij', q_dec, k_inv), 0.0)
    o_intra = jnp.einsum('bnhij,bnjhv->bnihv', att, v)
    kv_new = jnp.einsum('bnjhk,bnjhv->nbhkv', k_out, v)
    decay = jnp.exp(last[:, :, 0]).transpose(1, 0, 2, 3)

    def step(state, inp):
        dec, upd = inp
        return state * dec[..., None] + upd, state

    _, prev = lax.scan(step, jnp.zeros((b, h, dk, dv), jnp.float32), (decay, kv_new))
    o_inter = jnp.einsum('bnihk,nbhkv->bnihv', q_dec, prev)
    return (o_intra + o_inter).reshape(b, s, h, dv)


def _causal_dwconv(u, w, bias):
    s = u.shape[1]
    up = jnp.pad(u, ((0, 0), (CONV_W - 1, 0), (0, 0)))
    y = bias
    for kk in range(CONV_W):
        y = y + w[kk] * up[:, kk:kk + s]
    return y


def _layer(x, rel_bias, w_in, w_gla_gate, b_gla_gate, gla_norm, attn_sinks,
           w_br_a, w_br_b, w_br_c, w_out, g_pre_mix, g_post_mix, g_pre_ffn, g_post_ffn,
           w_up, conv_w, conv_b, w_down):
    b, s, _ = x.shape
    xn = _rmsnorm(x, g_pre_mix)
    proj = xn @ w_in
    splits = np.cumsum(IN_WIDTHS)[:-1].tolist()
    (aq, ak, av, bq, bk, bv, br, bg, cq, ck, cv, ga, gb, gc) = jnp.split(proj, splits, axis=-1)

    aq = aq.reshape(b, s, A_HEADS, HEAD_DIM)
    ak = ak.reshape(b, s, A_HEADS, HEAD_DIM)
    av = av.reshape(b, s, A_HEADS, HEAD_DIM)
    outs, lses = [], []
    for gi, (window, dilation) in enumerate(DIL_GROUPS):
        hs = slice(gi * A_HEADS_PER_GROUP, (gi + 1) * A_HEADS_PER_GROUP)
        o, l = _dilated_group(aq[:, :, hs], ak[:, :, hs], av[:, :, hs], rel_bias[:, hs], window, dilation)
        outs.append(o)
        lses.append(l)
    alpha = jax.nn.softmax(jnp.stack(lses), axis=0)
    o_a = jnp.sum(alpha[..., None] * jnp.stack(outs), axis=0).astype(x.dtype).reshape(b, s, A_OUT)

    log_a = jax.nn.log_sigmoid((bg @ w_gla_gate + b_gla_gate).astype(jnp.float32)) / B_GATE_TAU
    o_b = _gla(bq.reshape(b, s, B_HEADS, B_DK).astype(jnp.float32),
               bk.reshape(b, s, B_HEADS, B_DK).astype(jnp.float32),
               bv.reshape(b, s, B_HEADS, B_DV).astype(jnp.float32),
               log_a.reshape(b, s, B_HEADS, B_DK))
    o_b = _rmsnorm(o_b, gla_norm) * jax.nn.silu(br.astype(jnp.float32)).reshape(b, s, B_HEADS, B_DV)
    o_b = o_b.astype(x.dtype).reshape(b, s, B_OUT)

    o_c = _swa_sinks(cq.reshape(b, s, C_Q_HEADS, HEAD_DIM),
                     ck.reshape(b, s, C_KV_HEADS, HEAD_DIM),
                     cv.reshape(b, s, C_KV_HEADS, HEAD_DIM),
                     rel_bias[:, A_HEADS:], attn_sinks).astype(x.dtype)

    merged = (jax.nn.sigmoid(ga) * (o_a @ w_br_a)
              + jax.nn.sigmoid(gb) * (o_b @ w_br_b)
              + jax.nn.sigmoid(gc) * (o_c @ w_br_c))
    x = x + _rmsnorm(merged @ w_out, g_post_mix)

    hn = _rmsnorm(x, g_pre_ffn)
    u = _causal_dwconv(hn @ w_up, conv_w, conv_b)
    gate, up = jnp.split(u, 2, axis=-1)
    f = (jax.nn.silu(gate) * up) @ w_down
    return x + _rmsnorm(f, g_post_ffn)


def setup_inputs(seed: int = 0) -> dict:
    key = jax.random.key(seed)
    ks = jax.random.split(key, 20)
    d = D_MODEL

    def nrm(k, shape, scale):
        return jax.random.normal(k, shape, jnp.float32) * scale

    return {
        "x": nrm(ks[0], (BATCH, SEQ, d), 1.0),
        "rel_bias": nrm(ks[1], (REL_BUCKETS, REL_HEADS), 0.5),
        "w_in": nrm(ks[2], (DEPTH, d, N_IN), d ** -0.5),
        "w_gla_gate": nrm(ks[3], (DEPTH, B_GATE_RANK, B_HEADS * B_DK), B_GATE_RANK ** -0.5),
        "b_gla_gate": nrm(ks[4], (DEPTH, B_HEADS * B_DK), 0.1),
        "gla_norm": 1.0 + nrm(ks[5], (DEPTH, B_DV), 0.02),
        "attn_sinks": nrm(ks[6], (DEPTH, C_Q_HEADS), 0.5),
        "w_br_a": nrm(ks[7], (DEPTH, A_OUT, d), A_OUT ** -0.5),
        "w_br_b": nrm(ks[8], (DEPTH, B_OUT, d), B_OUT ** -0.5),
        "w_br_c": nrm(ks[9], (DEPTH, C_OUT, d), C_OUT ** -0.5),
        "w_out": nrm(ks[10], (DEPTH, d, d), d ** -0.5),
        "g_pre_mix": 1.0 + nrm(ks[11], (DEPTH, d), 0.02),
        "g_post_mix": 1.0 + nrm(ks[12], (DEPTH, d), 0.02),
        "g_pre_ffn": 1.0 + nrm(ks[13], (DEPTH, d), 0.02),
        "g_post_ffn": 1.0 + nrm(ks[14], (DEPTH, d), 0.02),
        "w_up": nrm(ks[15], (DEPTH, d, 2 * D_FF), d ** -0.5),
        "conv_w": nrm(ks[16], (DEPTH, CONV_W, 2 * D_FF), CONV_W ** -0.5),
        "conv_b": nrm(ks[17], (DEPTH, 2 * D_FF), 0.01),
        "w_down": nrm(ks[18], (DEPTH, D_FF, d), D_FF ** -0.5),
    }


def reference(x, rel_bias, w_in, w_gla_gate, b_gla_gate, gla_norm, attn_sinks,
              w_br_a, w_br_b, w_br_c, w_out, g_pre_mix, g_post_mix, g_pre_ffn, g_post_ffn,
              w_up, conv_w, conv_b, w_down):
    h = x
    for l in range(DEPTH):
        h = _layer(h, rel_bias, w_in[l], w_gla_gate[l], b_gla_gate[l], gla_norm[l], attn_sinks[l],
                   w_br_a[l], w_br_b[l], w_br_c[l], w_out[l], g_pre_mix[l], g_post_mix[l],
                   g_pre_ffn[l], g_post_ffn[l], w_up[l], conv_w[l], conv_b[l], w_down[l])
    return h
```

```python
import functools
import math

import numpy as np
import jax
import jax.numpy as jnp
from jax import lax
from jax.experimental import pallas as pl
from jax.experimental.pallas import tpu as pltpu

F32 = jnp.float32
BF16 = jnp.bfloat16

D_MODEL = 4096
HEAD_DIM = 128
BLOCK = 128
NORM_EPS = 1e-6
DIL_GROUPS = ((128, 1), (512, 4), (2048, 16))
A_HEADS_PER_GROUP = 4
A_HEADS = 12
A_OUT = 512
B_HEADS = 8
B_DK = 64
B_DV = 128
B_GATE_RANK = 16
B_GATE_TAU = 16.0
B_CHUNK = 64
B_OUT = 1024
C_Q_HEADS = 12
C_KV_HEADS = 3
C_WINDOW = 128
C_OUT = 1536
REL_BUCKETS = 32
REL_MAX_DIST = 2048
D_FF = 11008
CONV_W = 3

A_COLS = 3 * A_HEADS * HEAD_DIM
B_COLS = 2 * B_HEADS * B_DK + 2 * B_HEADS * B_DV
C_COLS = (C_Q_HEADS + 2 * C_KV_HEADS) * HEAD_DIM
G_COLS = 3 * D_MODEL
BG_PAD = 128
OFF_A = 0
OFF_B = OFF_A + A_COLS
OFF_C = OFF_B + B_COLS
OFF_G = OFF_C + C_COLS
OFF_BG = OFF_G + G_COLS
N_IN_P = OFF_BG + BG_PAD

D_FF_P = 11264
HALO = 16

VMEM_LIMIT = 56 * 1024 * 1024


def _cparams(sem):
    return pltpu.CompilerParams(dimension_semantics=sem, vmem_limit_bytes=VMEM_LIMIT)


def _rms(x, g):
    ms = jnp.mean(x * x, axis=-1, keepdims=True)
    return x * lax.rsqrt(ms + NORM_EPS) * g


def _rms_cast_kernel(x_ref, g_ref, o_ref):
    o_ref[...] = _rms(x_ref[...], g_ref[...]).astype(o_ref.dtype)


def rms_cast(x, g, *, tr=256):
    m, d = x.shape
    return pl.pallas_call(
        _rms_cast_kernel,
        grid=(m // tr,),
        in_specs=[pl.BlockSpec((tr, d), lambda i: (i, 0)),
                  pl.BlockSpec((1, d), lambda i: (0, 0))],
        out_specs=pl.BlockSpec((tr, d), lambda i: (i, 0)),
        out_shape=jax.ShapeDtypeStruct((m, d), BF16),
        compiler_params=_cparams(("parallel",)),
        name="rms_cast",
    )(x, g.reshape(1, d))


def _norm_res_kernel(y_ref, x_ref, g_ref, g2_ref, xo_ref, hn_ref):
    xn = x_ref[...] + _rms(y_ref[...].astype(F32), g_ref[...])
    xo_ref[...] = xn
    hn_ref[...] = _rms(xn, g2_ref[...]).astype(hn_ref.dtype)


def _norm_res_last_kernel(y_ref, x_ref, g_ref, xo_ref):
    xo_ref[...] = x_ref[...] + _rms(y_ref[...].astype(F32), g_ref[...])


def norm_residual(y, x, g, g_next=None, *, tr=256):
    m, d = x.shape
    row = pl.BlockSpec((tr, d), lambda i: (i, 0))
    vec = pl.BlockSpec((1, d), lambda i: (0, 0))
    if g_next is None:
        return pl.pallas_call(
            _norm_res_last_kernel, grid=(m // tr,),
            in_specs=[row, row, vec], out_specs=row,
            out_shape=jax.ShapeDtypeStruct((m, d), F32),
            compiler_params=_cparams(("parallel",)), name="norm_res_last",
        )(y, x, g.reshape(1, d)), None
    return pl.pallas_call(
        _norm_res_kernel, grid=(m // tr,),
        in_specs=[row, row, vec, vec], out_specs=[row, row],
        out_shape=[jax.ShapeDtypeStruct((m, d), F32), jax.ShapeDtypeStruct((m, d), BF16)],
        compiler_params=_cparams(("parallel",)), name="norm_res",
    )(y, x, g.reshape(1, d), g_next.reshape(1, d))


def _mm_kernel(a_ref, w_ref, o_ref, *, act):
    acc = jnp.dot(a_ref[...], w_ref[...], preferred_element_type=F32)
    if act == "sigmoid":
        acc = jax.nn.sigmoid(acc)
    o_ref[...] = acc.astype(o_ref.dtype)


def matmul(a, w, *, off=0, n=None, tm=1024, tn=1024, out_dtype=BF16, act=None, name="mm"):
    m, k = a.shape
    n = w.shape[1] if n is None else n
    tm = min(tm, m)
    assert m % tm == 0 and n % tn == 0 and off % tn == 0
    joff = off // tn
    return pl.pallas_call(
        functools.partial(_mm_kernel, act=act),
        grid=(m // tm, n // tn),
        in_specs=[pl.BlockSpec((tm, k), lambda i, j: (i, 0)),
                  pl.BlockSpec((k, tn), lambda i, j: (0, joff + j))],
        out_specs=pl.BlockSpec((tm, tn), lambda i, j: (i, j)),
        out_shape=jax.ShapeDtypeStruct((m, n), out_dtype),
        compiler_params=_cparams(("parallel", "arbitrary")),
        name=name,
    )(a, w)


def _mmk_kernel(a_ref, w_ref, o_ref, acc_ref):
    kk = pl.program_id(2)

    @pl.when(kk == 0)
    def _():
        acc_ref[...] = jnp.zeros_like(acc_ref)

    acc_ref[...] += jnp.dot(a_ref[...], w_ref[...], preferred_element_type=F32)

    @pl.when(kk == pl.num_programs(2) - 1)
    def _():
        o_ref[...] = acc_ref[...].astype(o_ref.dtype)


def matmul_k(a, w, *, tm=1024, tn=1024, tk=2816, out_dtype=F32, name="mmk"):
    m, k = a.shape
    n = w.shape[1]
    tm = min(tm, m)
    assert m % tm == 0 and n % tn == 0 and k % tk == 0
    return pl.pallas_call(
        _mmk_kernel,
        grid=(m // tm, n // tn, k // tk),
        in_specs=[pl.BlockSpec((tm, tk), lambda i, j, kk: (i, kk)),
                  pl.BlockSpec((tk, tn), lambda i, j, kk: (kk, j))],
        out_specs=pl.BlockSpec((tm, tn), lambda i, j, kk: (i, j)),
        out_shape=jax.ShapeDtypeStruct((m, n), out_dtype),
        scratch_shapes=[pltpu.VMEM((tm, tn), F32)],
        compiler_params=_cparams(("parallel", "arbitrary", "arbitrary")),
        name=name,
    )(a, w)


def _merge_kernel(oa_ref, ob_ref, oc_ref, ga_ref, gb_ref, gc_ref, wa_ref, wb_ref, wc_ref, o_ref):
    acc = ga_ref[...].astype(F32) * jnp.dot(oa_ref[...], wa_ref[...], preferred_element_type=F32)
    acc = acc + gb_ref[...].astype(F32) * jnp.dot(ob_ref[...], wb_ref[...], preferred_element_type=F32)
    acc = acc + gc_ref[...].astype(F32) * jnp.dot(oc_ref[...], wc_ref[...], preferred_element_type=F32)
    o_ref[...] = acc.astype(o_ref.dtype)


def merge(o_a, o_b, o_c, gates, wa, wb, wc, *, tm=1024, tn=1024):
    m = o_a.shape[0]
    d = wa.shape[1]
    tm = min(tm, m)
    nj = d // tn
    act = lambda kdim: pl.BlockSpec((tm, kdim), lambda i, j: (i, 0))
    gate = lambda s: pl.BlockSpec((tm, tn), lambda i, j: (i, s * nj + j))
    wgt = lambda kdim: pl.BlockSpec((kdim, tn), lambda i, j: (0, j))
    return pl.pallas_call(
        _merge_kernel,
        grid=(m // tm, nj),
        in_specs=[act(o_a.shape[1]), act(o_b.shape[1]), act(o_c.shape[1]),
                  gate(0), gate(1), gate(2),
                  wgt(wa.shape[0]), wgt(wb.shape[0]), wgt(wc.shape[0])],
        out_specs=pl.BlockSpec((tm, tn), lambda i, j: (i, j)),
        out_shape=jax.ShapeDtypeStruct((m, d), BF16),
        compiler_params=_cparams(("parallel", "arbitrary")),
        name="merge",
    )(o_a, o_b, o_c, gates, gates, gates, wa, wb, wc)


def _ffn_up_kernel(hn_ref, halo_ref, wg_ref, wu_ref, cwg_ref, cwu_ref, cbg_ref, cbu_ref, o_ref, *, tiles_per_seq):
    tm = hn_ref.shape[0]
    first = (pl.program_id(0) % tiles_per_seq) == 0
    halo = halo_ref[...]
    halo = jnp.where(first, jnp.zeros_like(halo), halo)
    a = jnp.concatenate([halo, hn_ref[...]], axis=0)

    def conv(w_ref, cw_ref, cb_ref):
        u = jnp.dot(a, w_ref[...], preferred_element_type=F32)
        cw = cw_ref[...]
        y = cb_ref[...] + cw[0:1] * u[HALO - 2:HALO - 2 + tm]
        y = y + cw[1:2] * u[HALO - 1:HALO - 1 + tm]
        return y + cw[2:3] * u[HALO:HALO + tm]

    gate = conv(wg_ref, cwg_ref, cbg_ref)
    up = conv(wu_ref, cwu_ref, cbu_ref)
    o_ref[...] = (jax.nn.silu(gate) * up).astype(o_ref.dtype)


def ffn_up(hn, w_up, conv_w, conv_b, *, seq, tm=1024, tn=512):
    m, d = hn.shape
    ffp = w_up.shape[1] // 2
    tm = min(tm, seq)
    assert seq % tm == 0 and ffp % tn == 0 and tm % HALO == 0
    nj = ffp // tn
    hb = tm // HALO
    return pl.pallas_call(
        functools.partial(_ffn_up_kernel, tiles_per_seq=seq // tm),
        grid=(m // tm, nj),
        in_specs=[pl.BlockSpec((tm, d), lambda i, j: (i, 0)),
                  pl.BlockSpec((HALO, d), lambda i, j: (jnp.maximum(i * hb - 1, 0), 0)),
                  pl.BlockSpec((d, tn), lambda i, j: (0, j)),
                  pl.BlockSpec((d, tn), lambda i, j: (0, nj + j)),
                  pl.BlockSpec((CONV_W, tn), lambda i, j: (0, j)),
                  pl.BlockSpec((CONV_W, tn), lambda i, j: (0, nj + j)),
                  pl.BlockSpec((1, tn), lambda i, j: (0, j)),
                  pl.BlockSpec((1, tn), lambda i, j: (0, nj + j))],
        out_specs=pl.BlockSpec((tm, tn), lambda i, j: (i, j)),
        out_shape=jax.ShapeDtypeStruct((m, ffp), BF16),
        compiler_params=_cparams(("parallel", "arbitrary")),
        name="ffn_up",
    )(hn, hn, w_up, w_up, conv_w, conv_w, conv_b, conv_b)


def _t5_bucket(dist):
    dist = jnp.maximum(dist, 0)
    max_exact = REL_BUCKETS // 2
    far = max_exact + (jnp.log(jnp.maximum(dist, 1).astype(F32) / max_exact)
                       / math.log(REL_MAX_DIST / max_exact) * (REL_BUCKETS - max_exact)).astype(jnp.int32)
    return jnp.where(dist < max_exact, dist, jnp.minimum(far, REL_BUCKETS - 1))


def _band_mask(nb, max_steps):
    qi = jnp.arange(BLOCK)[:, None]
    kj = jnp.arange(2 * BLOCK)[None, :]
    rel = qi + BLOCK - kj
    band = (rel >= 0) & (rel <= max_steps)
    valid = (jnp.arange(nb)[:, None, None] > 0) | (kj >= BLOCK)[None]
    return rel, band[None] & valid


def _with_prev_block(t, axis):
    pad = [(0, 0)] * t.ndim
    pad[axis] = (1, 0)
    prev = lax.slice_in_dim(jnp.pad(t, pad), 0, t.shape[axis], axis=axis)
    return jnp.concatenate([prev, t], axis=axis + 1)


def _dilated_group(q, k, v, bias_tab, window, dilation):
    b, s, h, e = q.shape
    span = window // dilation
    L = s // dilation
    nb = -(-L // BLOCK)
    Lp = nb * BLOCK

    def to_stream(t):
        t = t.reshape(b, L, dilation, h, e).transpose(0, 2, 1, 3, 4)
        t = jnp.pad(t, ((0, 0), (0, 0), (0, Lp - L), (0, 0), (0, 0)))
        return t.reshape(b, dilation, nb, BLOCK, h, e)

    qs = to_stream(q)
    kk = _with_prev_block(to_stream(k), 2)
    vv = _with_prev_block(to_stream(v), 2)
    rel, mask = _band_mask(nb, span)
    bias = bias_tab.astype(F32)[_t5_bucket(rel * dilation)].transpose(2, 0, 1)
    logits = jnp.einsum('bdnqhe,bdnkhe->bdnhqk', qs, kk, preferred_element_type=F32) * (e ** -0.5) + bias
    logits = jnp.where(mask[:, None], logits, -jnp.inf)
    lse = jax.nn.logsumexp(logits, axis=-1)
    p = jnp.exp(logits - lse[..., None])
    o = jnp.einsum('bdnhqk,bdnkhe->bdnqhe', p.astype(v.dtype), vv, preferred_element_type=F32)
    o = o.reshape(b, dilation, Lp, h, e)[:, :, :L].transpose(0, 2, 1, 3, 4).reshape(b, s, h, e)
    lse = lse.transpose(0, 1, 2, 4, 3).reshape(b, dilation, Lp, h)[:, :, :L]
    lse = lse.transpose(0, 2, 1, 3).reshape(b, s, h)
    return o, lse


def _swa_sinks(q, k, v, bias_tab, sinks):
    b, s, hq, e = q.shape
    hkv = k.shape[2]
    g = hq // hkv
    nb = s // BLOCK
    qb = q.reshape(b, nb, BLOCK, hkv, g, e)
    kk = _with_prev_block(k.reshape(b, nb, BLOCK, hkv, e), 1)
    vv = _with_prev_block(v.reshape(b, nb, BLOCK, hkv, e), 1)
    rel, mask = _band_mask(nb, C_WINDOW - 1)
    bias = bias_tab.astype(F32)[_t5_bucket(rel)].transpose(2, 0, 1).reshape(hkv, g, BLOCK, 2 * BLOCK)
    logits = jnp.einsum('bnqhge,bnkhe->bnhgqk', qb, kk, preferred_element_type=F32) * (e ** -0.5) + bias
    logits = jnp.where(mask[:, None, None], logits, -jnp.inf)
    sink = sinks.astype(F32).reshape(hkv, g, 1, 1)
    m = jnp.maximum(jnp.max(logits, axis=-1, keepdims=True), sink)
    w = jnp.exp(logits - m)
    p = w / (jnp.sum(w, axis=-1, keepdims=True) + jnp.exp(sink - m))
    o = jnp.einsum('bnhgqk,bnkhe->bnqhge', p.astype(v.dtype), vv, preferred_element_type=F32)
    return o.reshape(b, s, hq * e)


def _gla(q, k, v, log_a):
    b, s, h, dk = q.shape
    dv = v.shape[-1]
    nc = s // B_CHUNK
    q = q.reshape(b, nc, B_CHUNK, h, dk) * (dk ** -0.5)
    k = k.reshape(b, nc, B_CHUNK, h, dk)
    v = v.reshape(b, nc, B_CHUNK, h, dv)
    cum = lax.cumsum(log_a.reshape(b, nc, B_CHUNK, h, dk), axis=2)
    last = cum[:, :, -1:]
    q_dec = q * jnp.exp(cum)
    k_inv = k * jnp.exp(-cum)
    k_out = k * jnp.exp(last - cum)
    causal = jnp.tril(jnp.ones((B_CHUNK, B_CHUNK), dtype=bool))
    att = jnp.where(causal, jnp.einsum('bnihk,bnjhk->bnhij', q_dec, k_inv), 0.0)
    o_intra = jnp.einsum('bnhij,bnjhv->bnihv', att, v)
    kv_new = jnp.einsum('bnjhk,bnjhv->nbhkv', k_out, v)
    decay = jnp.exp(last[:, :, 0]).transpose(1, 0, 2, 3)

    def step(state, inp):
        dec, upd = inp
        return state * dec[..., None] + upd, state

    _, prev = lax.scan(step, jnp.zeros((b, h, dk, dv), F32), (decay, kv_new))
    o_inter = jnp.einsum('bnihk,nbhkv->bnihv', q_dec, prev)
    return (o_intra + o_inter).reshape(b, s, h, dv)


def _mixers_jax(pa, pb, pc, bg, rel_bias, w_gla_gate, b_gla_gate, gla_norm, attn_sinks, b, s):
    pa = pa.astype(F32).reshape(b, s, 3, A_HEADS, HEAD_DIM)
    aq, ak, av = pa[:, :, 0], pa[:, :, 1], pa[:, :, 2]
    outs, lses = [], []
    for gi, (window, dilation) in enumerate(DIL_GROUPS):
        hs = slice(gi * A_HEADS_PER_GROUP, (gi + 1) * A_HEADS_PER_GROUP)
        o, l = _dilated_group(aq[:, :, hs], ak[:, :, hs], av[:, :, hs], rel_bias[:, hs], window, dilation)
        outs.append(o)
        lses.append(l)
    alpha = jax.nn.softmax(jnp.stack(lses), axis=0)
    o_a = jnp.sum(alpha[..., None] * jnp.stack(outs), axis=0).reshape(b * s, A_OUT).astype(BF16)

    pb = pb.astype(F32).reshape(b, s, B_COLS)
    bq = pb[..., :512]
    bk = pb[..., 512:1024]
    bv = pb[..., 1024:2048]
    br = pb[..., 2048:]
    bgv = bg.reshape(b, s, BG_PAD)[..., :B_GATE_RANK]
    log_a = jax.nn.log_sigmoid(bgv @ w_gla_gate + b_gla_gate) / B_GATE_TAU
    o_b = _gla(bq.reshape(b, s, B_HEADS, B_DK), bk.reshape(b, s, B_HEADS, B_DK),
               bv.reshape(b, s, B_HEADS, B_DV), log_a.reshape(b, s, B_HEADS, B_DK))
    o_b = _rms(o_b, gla_norm) * jax.nn.silu(br).reshape(b, s, B_HEADS, B_DV)
    o_b = o_b.reshape(b * s, B_OUT).astype(BF16)

    pc = pc.astype(F32).reshape(b, s, C_COLS)
    o_c = _swa_sinks(pc[..., :1536].reshape(b, s, C_Q_HEADS, HEAD_DIM),
                     pc[..., 1536:1920].reshape(b, s, C_KV_HEADS, HEAD_DIM),
                     pc[..., 1920:].reshape(b, s, C_KV_HEADS, HEAD_DIM),
                     rel_bias[:, A_HEADS:], attn_sinks)
    o_c = o_c.reshape(b * s, C_OUT).astype(BF16)
    return o_a, o_b, o_c


def _prep_w_in(w):
    d = w.shape[0]
    lo = A_COLS + B_COLS - 0
    cols = jnp.concatenate([w[:, :lo], w[:, lo + B_GATE_RANK:], w[:, lo:lo + B_GATE_RANK],
                            jnp.zeros((d, BG_PAD - B_GATE_RANK), w.dtype)], axis=1)
    return cols.astype(BF16)


def _pad_ff(t, axis):
    pad = [(0, 0)] * t.ndim
    pad[axis] = (0, D_FF_P - D_FF)
    return jnp.pad(t, pad)


def _prep_up(t):
    g, u = t[..., :D_FF], t[..., D_FF:]
    return jnp.concatenate([_pad_ff(g, t.ndim - 1), _pad_ff(u, t.ndim - 1)], axis=-1)


def kernel(x, rel_bias, w_in, w_gla_gate, b_gla_gate, gla_norm, attn_sinks, w_br_a, w_br_b, w_br_c, w_out,
           g_pre_mix, g_post_mix, g_pre_ffn, g_post_ffn, w_up, conv_w, conv_b, w_down):
    b, s, d = x.shape
    depth = w_in.shape[0]
    m = b * s
    xf = x.reshape(m, d)
    xn = rms_cast(xf, g_pre_mix[0])
    for l in range(depth):
        w_in_l = _prep_w_in(w_in[l])
        pa = matmul(xn, w_in_l, off=OFF_A, n=A_COLS, tn=768, name="proj_a")
        pb = matmul(xn, w_in_l, off=OFF_B, n=B_COLS, tn=768, name="proj_b")
        pc = matmul(xn, w_in_l, off=OFF_C, n=C_COLS, tn=768, name="proj_c")
        gates = matmul(xn, w_in_l, off=OFF_G, n=G_COLS, tn=768, act="sigmoid", name="proj_g")
        bg = matmul(xn, w_in_l, off=OFF_BG, n=BG_PAD, tn=BG_PAD, out_dtype=F32, name="proj_bg")

        o_a, o_b, o_c = _mixers_jax(pa, pb, pc, bg, rel_bias, w_gla_gate[l], b_gla_gate[l], gla_norm[l],
                                    attn_sinks[l], b, s)

        merged = merge(o_a, o_b, o_c, gates, w_br_a[l].astype(BF16), w_br_b[l].astype(BF16),
                       w_br_c[l].astype(BF16))
        y = matmul(merged, w_out[l].astype(BF16), out_dtype=F32, name="proj_out")
        xf, hn = norm_residual(y, xf, g_post_mix[l], g_pre_ffn[l])

        h = ffn_up(hn, _prep_up(w_up[l]).astype(BF16), _prep_up(conv_w[l]),
                   _prep_up(conv_b[l]).reshape(1, 2 * D_FF_P), seq=s)
        f = matmul_k(h, _pad_ff(w_down[l], 0).astype(BF16), name="ffn_down")
        xf, xn = norm_residual(f, xf, g_post_ffn[l], g_pre_mix[l + 1] if l + 1 < depth else None)
    return xf.reshape(b, s, d)
```

```python
import functools
import math

import jax
import jax.numpy as jnp
from jax import lax
from jax.experimental import pallas as pl
from jax.experimental.pallas import tpu as pltpu

F32 = jnp.float32
BF16 = jnp.bfloat16

D_MODEL = 4096
HEAD_DIM = 128
BLOCK = 128
NORM_EPS = 1e-6
DIL_GROUPS = ((128, 1), (512, 4), (2048, 16))
A_HEADS_PER_GROUP = 4
A_HEADS = 12
A_OUT = 512
B_HEADS = 8
B_DK = 64
B_DV = 128
B_GATE_RANK = 16
B_GATE_TAU = 16.0
B_CHUNK = 64
B_OUT = 1024
C_Q_HEADS = 12
C_KV_HEADS = 3
C_WINDOW = 128
C_OUT = 1536
REL_BUCKETS = 32
REL_MAX_DIST = 2048
D_FF = 11008
CONV_W = 3

A_COLS = 3 * A_HEADS * HEAD_DIM
B_COLS = 2 * B_HEADS * B_DK + 2 * B_HEADS * B_DV
C_COLS = (C_Q_HEADS + 2 * C_KV_HEADS) * HEAD_DIM
G_COLS = 3 * D_MODEL
BG_PAD = 128
OFF_B = A_COLS
OFF_BG = OFF_B + B_COLS
OFF_C = OFF_BG + B_GATE_RANK
OFF_G = OFF_C + C_COLS

D_FF_P = 11264
HALO = 16

VMEM_LIMIT = 56 * 1024 * 1024


def _cparams(sem):
    return pltpu.CompilerParams(dimension_semantics=sem, vmem_limit_bytes=VMEM_LIMIT)


def _rms(x, g):
    ms = jnp.mean(x * x, axis=-1, keepdims=True)
    return x * lax.rsqrt(ms + NORM_EPS) * g


def _rms_cast_kernel(x_ref, g_ref, o_ref):
    o_ref[...] = _rms(x_ref[...], g_ref[...]).astype(o_ref.dtype)


def rms_cast(x, g, *, tr=256):
    m, d = x.shape
    return pl.pallas_call(
        _rms_cast_kernel,
        grid=(m // tr,),
        in_specs=[pl.BlockSpec((tr, d), lambda i: (i, 0)),
                  pl.BlockSpec((1, d), lambda i: (0, 0))],
        out_specs=pl.BlockSpec((tr, d), lambda i: (i, 0)),
        out_shape=jax.ShapeDtypeStruct((m, d), BF16),
        compiler_params=_cparams(("parallel",)),
        name="rms_cast",
    )(x, g.reshape(1, d))


def _norm_res_kernel(y_ref, x_ref, g_ref, g2_ref, xo_ref, hn_ref):
    xn = x_ref[...] + _rms(y_ref[...].astype(F32), g_ref[...])
    xo_ref[...] = xn
    hn_ref[...] = _rms(xn, g2_ref[...]).astype(hn_ref.dtype)


def _norm_res_last_kernel(y_ref, x_ref, g_ref, xo_ref):
    xo_ref[...] = x_ref[...] + _rms(y_ref[...].astype(F32), g_ref[...])


def norm_residual(y, x, g, g_next=None, *, tr=256):
    m, d = x.shape
    row = pl.BlockSpec((tr, d), lambda i: (i, 0))
    vec = pl.BlockSpec((1, d), lambda i: (0, 0))
    if g_next is None:
        return pl.pallas_call(
            _norm_res_last_kernel, grid=(m // tr,),
            in_specs=[row, row, vec], out_specs=row,
            out_shape=jax.ShapeDtypeStruct((m, d), F32),
            compiler_params=_cparams(("parallel",)), name="norm_res_last",
        )(y, x, g.reshape(1, d)), None
    return pl.pallas_call(
        _norm_res_kernel, grid=(m // tr,),
        in_specs=[row, row, vec, vec], out_specs=[row, row],
        out_shape=[jax.ShapeDtypeStruct((m, d), F32), jax.ShapeDtypeStruct((m, d), BF16)],
        compiler_params=_cparams(("parallel",)), name="norm_res",
    )(y, x, g.reshape(1, d), g_next.reshape(1, d))


def _mm_kernel(a_ref, w_ref, o_ref, *, act):
    acc = jnp.dot(a_ref[...], w_ref[...], preferred_element_type=F32)
    if act == "sigmoid":
        acc = jax.nn.sigmoid(acc)
    o_ref[...] = acc.astype(o_ref.dtype)


def matmul(a, w, *, tm=1024, tn=1024, out_dtype=BF16, act=None, name="mm"):
    m, k = a.shape
    n = w.shape[1]
    tm = min(tm, m)
    tn = min(tn, n)
    assert m % tm == 0 and n % tn == 0
    return pl.pallas_call(
        functools.partial(_mm_kernel, act=act),
        grid=(m // tm, n // tn),
        in_specs=[pl.BlockSpec((tm, k), lambda i, j: (i, 0)),
                  pl.BlockSpec((k, tn), lambda i, j: (0, j))],
        out_specs=pl.BlockSpec((tm, tn), lambda i, j: (i, j)),
        out_shape=jax.ShapeDtypeStruct((m, n), out_dtype),
        compiler_params=_cparams(("parallel", "arbitrary")),
        name=name,
    )(a, w)


def _mmk_kernel(a_ref, w_ref, o_ref, acc_ref):
    kk = pl.program_id(2)

    @pl.when(kk == 0)
    def _():
        acc_ref[...] = jnp.zeros_like(acc_ref)

    acc_ref[...] += jnp.dot(a_ref[...], w_ref[...], preferred_element_type=F32)

    @pl.when(kk == pl.num_programs(2) - 1)
    def _():
        o_ref[...] = acc_ref[...].astype(o_ref.dtype)


def matmul_k(a, w, *, tm=1024, tn=1024, tk=2816, out_dtype=F32, name="mmk"):
    m, k = a.shape
    n = w.shape[1]
    tm = min(tm, m)
    assert m % tm == 0 and n % tn == 0 and k % tk == 0
    return pl.pallas_call(
        _mmk_kernel,
        grid=(m // tm, n // tn, k // tk),
        in_specs=[pl.BlockSpec((tm, tk), lambda i, j, kk: (i, kk)),
                  pl.BlockSpec((tk, tn), lambda i, j, kk: (kk, j))],
        out_specs=pl.BlockSpec((tm, tn), lambda i, j, kk: (i, j)),
        out_shape=jax.ShapeDtypeStruct((m, n), out_dtype),
        scratch_shapes=[pltpu.VMEM((tm, tn), F32)],
        compiler_params=_cparams(("parallel", "arbitrary", "arbitrary")),
        name=name,
    )(a, w)


def _merge_kernel(oa_ref, ob_ref, oc_ref, ga_ref, gb_ref, gc_ref, wa_ref, wb_ref, wc_ref, o_ref):
    acc = ga_ref[...].astype(F32) * jnp.dot(oa_ref[...], wa_ref[...], preferred_element_type=F32)
    acc = acc + gb_ref[...].astype(F32) * jnp.dot(ob_ref[...], wb_ref[...], preferred_element_type=F32)
    acc = acc + gc_ref[...].astype(F32) * jnp.dot(oc_ref[...], wc_ref[...], preferred_element_type=F32)
    o_ref[...] = acc.astype(o_ref.dtype)


def merge(o_a, o_b, o_c, gates, wa, wb, wc, *, tm=1024, tn=1024):
    m = o_a.shape[0]
    d = wa.shape[1]
    tm = min(tm, m)
    nj = d // tn
    act = lambda kdim: pl.BlockSpec((tm, kdim), lambda i, j: (i, 0))
    gate = lambda s: pl.BlockSpec((tm, tn), lambda i, j: (i, s * nj + j))
    wgt = lambda kdim: pl.BlockSpec((kdim, tn), lambda i, j: (0, j))
    return pl.pallas_call(
        _merge_kernel,
        grid=(m // tm, nj),
        in_specs=[act(o_a.shape[1]), act(o_b.shape[1]), act(o_c.shape[1]),
                  gate(0), gate(1), gate(2),
                  wgt(wa.shape[0]), wgt(wb.shape[0]), wgt(wc.shape[0])],
        out_specs=pl.BlockSpec((tm, tn), lambda i, j: (i, j)),
        out_shape=jax.ShapeDtypeStruct((m, d), BF16),
        compiler_params=_cparams(("parallel", "arbitrary")),
        name="merge",
    )(o_a, o_b, o_c, gates, gates, gates, wa, wb, wc)


def _ffn_up_kernel(hn_ref, halo_ref, wg_ref, wu_ref, cwg_ref, cwu_ref, cbg_ref, cbu_ref, o_ref, *, tiles_per_seq):
    tm = hn_ref.shape[0]
    first = (pl.program_id(0) % tiles_per_seq) == 0
    halo = halo_ref[...]
    halo = jnp.where(first, jnp.zeros_like(halo), halo)
    a = jnp.concatenate([halo, hn_ref[...]], axis=0)

    def conv(w_ref, cw_ref, cb_ref):
        u = jnp.dot(a, w_ref[...], preferred_element_type=F32)
        cw = cw_ref[...]
        y = cb_ref[...] + cw[0:1] * u[HALO - 2:HALO - 2 + tm]
        y = y + cw[1:2] * u[HALO - 1:HALO - 1 + tm]
        return y + cw[2:3] * u[HALO:HALO + tm]

    gate = conv(wg_ref, cwg_ref, cbg_ref)
    up = conv(wu_ref, cwu_ref, cbu_ref)
    o_ref[...] = (jax.nn.silu(gate) * up).astype(o_ref.dtype)


def ffn_up(hn, w_gate, w_upper, conv_w, conv_b, *, seq, tm=1024, tn=512):
    m, d = hn.shape
    ffp = w_gate.shape[1]
    tm = min(tm, seq)
    assert seq % tm == 0 and ffp % tn == 0 and tm % HALO == 0
    nj = ffp // tn
    hb = tm // HALO
    return pl.pallas_call(
        functools.partial(_ffn_up_kernel, tiles_per_seq=seq // tm),
        grid=(m // tm, nj),
        in_specs=[pl.BlockSpec((tm, d), lambda i, j: (i, 0)),
                  pl.BlockSpec((HALO, d), lambda i, j: (jnp.maximum(i * hb - 1, 0), 0)),
                  pl.BlockSpec((d, tn), lambda i, j: (0, j)),
                  pl.BlockSpec((d, tn), lambda i, j: (0, j)),
                  pl.BlockSpec((CONV_W, tn), lambda i, j: (0, j)),
                  pl.BlockSpec((CONV_W, tn), lambda i, j: (0, nj + j)),
                  pl.BlockSpec((1, tn), lambda i, j: (0, j)),
                  pl.BlockSpec((1, tn), lambda i, j: (0, nj + j))],
        out_specs=pl.BlockSpec((tm, tn), lambda i, j: (i, j)),
        out_shape=jax.ShapeDtypeStruct((m, ffp), BF16),
        compiler_params=_cparams(("parallel", "arbitrary")),
        name="ffn_up",
    )(hn, hn, w_gate, w_upper, conv_w, conv_w, conv_b, conv_b)


NEG = -1e30
N_BIAS_HEADS = A_HEADS + C_Q_HEADS


def _bucket_tiles():
    qi = jnp.arange(BLOCK)[:, None]
    kj = jnp.arange(2 * BLOCK)[None, :]
    rel = qi + BLOCK - kj
    max_exact = REL_BUCKETS // 2
    tiles = []
    for dilation, span in [(d, w // d) for w, d in DIL_GROUPS] + [(1, C_WINDOW - 1)]:
        dist = jnp.maximum(rel * dilation, 0)
        far = max_exact + (jnp.log(jnp.maximum(dist, 1).astype(F32) / max_exact)
                           / math.log(REL_MAX_DIST / max_exact) * (REL_BUCKETS - max_exact)).astype(jnp.int32)
        bucket = jnp.where(dist < max_exact, dist, jnp.minimum(far, REL_BUCKETS - 1))
        tiles.append(jnp.where((rel >= 0) & (rel <= span), bucket, -1))
    return jnp.stack(tiles).astype(jnp.int32)


def _bias_kernel(tab_ref, idx_ref, o_ref):
    h = pl.program_id(0)
    idx = idx_ref[...]
    acc = jnp.full(idx.shape, NEG, F32)
    for bkt in range(REL_BUCKETS):
        acc = jnp.where(idx == bkt, tab_ref[bkt, h], acc)
    col = lax.broadcasted_iota(jnp.int32, idx.shape, 1)
    o_ref[0] = jnp.where(col < BLOCK, NEG, acc)
    o_ref[1] = acc


def bias_tiles(rel_bias):
    groups = len(DIL_GROUPS)
    return pl.pallas_call(
        _bias_kernel,
        grid=(N_BIAS_HEADS,),
        in_specs=[pl.BlockSpec(memory_space=pltpu.SMEM),
                  pl.BlockSpec((None, BLOCK, 2 * BLOCK),
                               lambda h: (jnp.minimum(h // A_HEADS_PER_GROUP, groups), 0, 0))],
        out_specs=pl.BlockSpec((2, None, BLOCK, 2 * BLOCK), lambda h: (0, h, 0, 0)),
        out_shape=jax.ShapeDtypeStruct((2, N_BIAS_HEADS, BLOCK, 2 * BLOCK), F32),
        compiler_params=_cparams(("arbitrary",)),
        name="bias_tiles",
    )(rel_bias, _bucket_tiles())


ATT_SCALE = HEAD_DIM ** -0.5
HEADS_PER_STEP = 4
STEP_COLS = HEADS_PER_STEP * HEAD_DIM


def _scores(q, kp, kc, bias):
    k = jnp.concatenate([kp, kc], axis=0)
    s = lax.dot_general(q, k, (((1,), (1,)), ((), ())), preferred_element_type=F32)
    return s * ATT_SCALE + bias


def _attn_a_kernel(q_ref, kc_ref, kp_ref, vc_ref, vp_ref, bias_ref, o_ref, lse_ref):
    for h in range(HEADS_PER_STEP):
        sl = slice(h * HEAD_DIM, (h + 1) * HEAD_DIM)
        s = _scores(q_ref[:, sl], kp_ref[:, sl], kc_ref[:, sl], bias_ref[h])
        m = jnp.max(s, axis=-1, keepdims=True)
        p = jnp.exp(s - m)
        l = jnp.sum(p, axis=-1, keepdims=True)
        v = jnp.concatenate([vp_ref[:, sl], vc_ref[:, sl]], axis=0)
        o = jnp.dot(p.astype(BF16), v, preferred_element_type=F32)
        o_ref[:, sl] = o / l
        lse_ref[:, sl] = jnp.broadcast_to(m + jnp.log(l), (BLOCK, HEAD_DIM))


def attn_dilated_group(pa, bias, gi, dilation, b, s):
    L = s // dilation
    nb = L // BLOCK
    ncol = A_COLS // STEP_COLS
    hg = A_HEADS // HEADS_PER_STEP
    pav = pa.reshape(b, L, dilation * A_COLS)

    def cur(sec):
        return pl.BlockSpec((None, BLOCK, STEP_COLS), lambda bi, r, n: (bi, n, r * ncol + sec * hg + gi))

    def prev(sec):
        return pl.BlockSpec((None, BLOCK, STEP_COLS),
                            lambda bi, r, n: (bi, jnp.maximum(n - 1, 0), r * ncol + sec * hg + gi))

    out_spec = pl.BlockSpec((None, BLOCK, STEP_COLS), lambda bi, r, n: (bi, n, r))
    out_sds = jax.ShapeDtypeStruct((b, L, dilation * STEP_COLS), F32)
    o, lse = pl.pallas_call(
        _attn_a_kernel,
        grid=(b, dilation, nb),
        in_specs=[cur(0), cur(1), prev(1), cur(2), prev(2),
                  pl.BlockSpec((None, HEADS_PER_STEP, BLOCK, 2 * BLOCK),
                               lambda bi, r, n: (jnp.minimum(n, 1), gi, 0, 0))],
        out_specs=[out_spec, out_spec],
        out_shape=[out_sds, out_sds],
        compiler_params=_cparams(("parallel", "parallel", "arbitrary")),
        name=f"attn_dil{dilation}",
    )(pav, pav, pav, pav, pav, bias)
    return o.reshape(b * s, STEP_COLS), lse.reshape(b * s, STEP_COLS)


def _lse_mix_kernel(o0_ref, o1_ref, o2_ref, l0_ref, l1_ref, l2_ref, o_ref):
    l0, l1, l2 = l0_ref[...], l1_ref[...], l2_ref[...]
    mx = jnp.maximum(jnp.maximum(l0, l1), l2)
    e0, e1, e2 = jnp.exp(l0 - mx), jnp.exp(l1 - mx), jnp.exp(l2 - mx)
    num = e0 * o0_ref[...] + e1 * o1_ref[...] + e2 * o2_ref[...]
    o_ref[...] = (num / (e0 + e1 + e2)).astype(o_ref.dtype)


def lse_mix(outs, lses, *, tr=1024):
    m, c = outs[0].shape
    tr = min(tr, m)
    spec = pl.BlockSpec((tr, c), lambda i: (i, 0))
    return pl.pallas_call(
        _lse_mix_kernel, grid=(m // tr,),
        in_specs=[spec] * 6, out_specs=spec,
        out_shape=jax.ShapeDtypeStruct((m, c), BF16),
        compiler_params=_cparams(("parallel",)), name="lse_mix",
    )(*outs, *lses)


def _attn_c_kernel(sink_ref, q_ref, kc_ref, kp_ref, vc_ref, vp_ref, bias_ref, o_ref):
    kh = pl.program_id(2)
    v = jnp.concatenate([vp_ref[...], vc_ref[...]], axis=0)
    for gq in range(HEADS_PER_STEP):
        sl = slice(gq * HEAD_DIM, (gq + 1) * HEAD_DIM)
        s = _scores(q_ref[:, sl], kp_ref[...], kc_ref[...], bias_ref[gq])
        sink = sink_ref[kh * HEADS_PER_STEP + gq]
        m = jnp.maximum(jnp.max(s, axis=-1, keepdims=True), sink)
        w = jnp.exp(s - m)
        den = jnp.sum(w, axis=-1, keepdims=True) + jnp.exp(sink - m)
        o = jnp.dot(w.astype(BF16), v, preferred_element_type=F32)
        o_ref[:, sl] = (o / den).astype(o_ref.dtype)


def attn_swa(pc, bias, sinks, b, s):
    nb = s // BLOCK
    pcv = pc.reshape(b, s, C_COLS)
    koff = C_Q_HEADS
    voff = C_Q_HEADS + C_KV_HEADS

    def kv(off, back):
        return pl.BlockSpec((None, BLOCK, HEAD_DIM),
                            lambda bi, n, kh: (bi, jnp.maximum(n - back, 0), off + kh))

    o = pl.pallas_call(
        _attn_c_kernel,
        grid=(b, nb, C_KV_HEADS),
        in_specs=[pl.BlockSpec(memory_space=pltpu.SMEM),
                  pl.BlockSpec((None, BLOCK, STEP_COLS), lambda bi, n, kh: (bi, n, kh)),
                  kv(koff, 0), kv(koff, 1), kv(voff, 0), kv(voff, 1),
                  pl.BlockSpec((None, HEADS_PER_STEP, BLOCK, 2 * BLOCK),
                               lambda bi, n, kh: (jnp.minimum(n, 1), A_HEADS // HEADS_PER_STEP + kh, 0, 0))],
        out_specs=pl.BlockSpec((None, BLOCK, STEP_COLS), lambda bi, n, kh: (bi, n, kh)),
        out_shape=jax.ShapeDtypeStruct((b, s, C_OUT), BF16),
        compiler_params=_cparams(("parallel", "parallel", "arbitrary")),
        name="attn_swa",
    )(sinks, pcv, pcv, pcv, pcv, pcv, bias)
    return o.reshape(b * s, C_OUT)


B_QK = B_HEADS * B_DK


def _split3(x):
    hi = x.astype(BF16)
    r1 = x - hi.astype(F32)
    mid = r1.astype(BF16)
    lo = (r1 - mid.astype(F32)).astype(BF16)
    return hi, mid, lo


def _gla_kernel(q_ref, k_ref, v_ref, r_ref, bg_ref, wg_ref, bgate_ref, gnorm_ref, o_ref, state_ref):
    @pl.when(pl.program_id(1) == 0)
    def _():
        state_ref[...] = jnp.zeros_like(state_ref)

    z = jnp.dot(bg_ref[...].astype(BF16), wg_ref[...], preferred_element_type=F32) + bgate_ref[...]
    log_a = (jnp.minimum(z, 0.0) - jnp.log(1.0 + jnp.exp(-jnp.abs(z)))) / B_GATE_TAU

    row = lax.broadcasted_iota(jnp.int32, (B_CHUNK, B_CHUNK), 0)
    coli = lax.broadcasted_iota(jnp.int32, (B_CHUNK, B_CHUNK), 1)
    causal = row >= coli
    tril = jnp.where(causal, 1.0, 0.0).astype(BF16)
    cum = sum(jnp.dot(tril, part, preferred_element_type=F32) for part in _split3(log_a))
    last = cum[B_CHUNK - 1:B_CHUNK, :]

    qf = q_ref[...].astype(F32) * (B_DK ** -0.5)
    kf = k_ref[...].astype(F32)
    q_dec = (qf * jnp.exp(cum)).astype(BF16)
    k_inv = (kf * jnp.exp(-cum)).astype(BF16)
    k_out = (kf * jnp.exp(last - cum)).astype(BF16)
    decay = jnp.exp(last)

    for h in range(B_HEADS):
        ks = slice(h * B_DK, (h + 1) * B_DK)
        vs = slice(h * B_DV, (h + 1) * B_DV)
        qd, ki, ko, vh = q_dec[:, ks], k_inv[:, ks], k_out[:, ks], v_ref[:, vs]
        att = lax.dot_general(qd, ki, (((1,), (1,)), ((), ())), preferred_element_type=F32)
        att = jnp.where(causal, att, 0.0).astype(BF16)
        st = state_ref[h]
        o = jnp.dot(att, vh, preferred_element_type=F32)
        o = o + lax.dot_general(qd, st.astype(BF16), (((1,), (1,)), ((), ())), preferred_element_type=F32)
        upd = lax.dot_general(vh, ko, (((0,), (0,)), ((), ())), preferred_element_type=F32)
        state_ref[h] = st * decay[:, ks] + upd
        on = _rms(o, gnorm_ref[:, vs])
        o_ref[:, vs] = (on * jax.nn.silu(r_ref[:, vs].astype(F32))).astype(o_ref.dtype)


def gla(pb, bg, w_gate, b_gate, gnorm, b, s):
    nc = s // B_CHUNK
    pbv = pb.reshape(b, s, B_COLS)
    bgv = bg.reshape(b, s, BG_PAD)
    wg = jnp.pad(w_gate, ((0, BG_PAD - B_GATE_RANK), (0, 0))).astype(BF16)
    full = lambda shape: pl.BlockSpec(shape, lambda bi, c: (0, 0))
    o = pl.pallas_call(
        _gla_kernel,
        grid=(b, nc),
        in_specs=[pl.BlockSpec((None, B_CHUNK, B_QK), lambda bi, c: (bi, c, 0)),
                  pl.BlockSpec((None, B_CHUNK, B_QK), lambda bi, c: (bi, c, 1)),
                  pl.BlockSpec((None, B_CHUNK, B_OUT), lambda bi, c: (bi, c, 1)),
                  pl.BlockSpec((None, B_CHUNK, B_OUT), lambda bi, c: (bi, c, 2)),
                  pl.BlockSpec((None, B_CHUNK, BG_PAD), lambda bi, c: (bi, c, 0)),
                  full((BG_PAD, B_QK)), full((1, B_QK)), full((1, B_OUT))],
        out_specs=pl.BlockSpec((None, B_CHUNK, B_OUT), lambda bi, c: (bi, c, 0)),
        out_shape=jax.ShapeDtypeStruct((b, s, B_OUT), BF16),
        scratch_shapes=[pltpu.VMEM((B_HEADS, B_DV, B_DK), F32)],
        compiler_params=_cparams(("parallel", "arbitrary")),
        name="gla",
    )(pbv, pbv, pbv, pbv, bgv, wg, b_gate.reshape(1, B_QK), jnp.tile(gnorm, B_HEADS).reshape(1, B_OUT))
    return o.reshape(b * s, B_OUT)


def token_mixers(pa, pb, pc, bg, bias, w_gla_gate, b_gla_gate, gla_norm, attn_sinks, b, s):
    outs, lses = [], []
    for gi, (_, dilation) in enumerate(DIL_GROUPS):
        o, lse = attn_dilated_group(pa, bias, gi, dilation, b, s)
        outs.append(o)
        lses.append(lse)
    o_a = lse_mix(outs, lses)
    o_b = gla(pb, bg, w_gla_gate, b_gla_gate, gla_norm, b, s)
    o_c = attn_swa(pc, bias, attn_sinks, b, s)
    return o_a, o_b, o_c


def _cols(w, lo, n):
    return w[:, lo:lo + n].astype(BF16)


def _pad_ff(t, axis):
    pad = [(0, 0)] * t.ndim
    pad[axis] = (0, D_FF_P - D_FF)
    return jnp.pad(t, pad)


def _halves_padded(t):
    return jnp.concatenate([_pad_ff(t[..., :D_FF], t.ndim - 1), _pad_ff(t[..., D_FF:], t.ndim - 1)], axis=-1)


def kernel(x, rel_bias, w_in, w_gla_gate, b_gla_gate, gla_norm, attn_sinks, w_br_a, w_br_b, w_br_c, w_out,
           g_pre_mix, g_post_mix, g_pre_ffn, g_post_ffn, w_up, conv_w, conv_b, w_down):
    b, s, d = x.shape
    depth = w_in.shape[0]
    xf = x.reshape(b * s, d)
    xn = rms_cast(xf, g_pre_mix[0])
    bias = bias_tiles(rel_bias)
    for l in range(depth):
        w = w_in[l]
        pa = matmul(xn, _cols(w, 0, A_COLS), tn=768, name="proj_a")
        pb = matmul(xn, _cols(w, OFF_B, B_COLS), tn=768, name="proj_b")
        pc = matmul(xn, _cols(w, OFF_C, C_COLS), tn=768, name="proj_c")
        gates = matmul(xn, _cols(w, OFF_G, G_COLS), tn=768, act="sigmoid", name="proj_g")
        w_bg = jnp.pad(_cols(w, OFF_BG, B_GATE_RANK), ((0, 0), (0, BG_PAD - B_GATE_RANK)))
        bg = matmul(xn, w_bg, out_dtype=F32, name="proj_bg")

        o_a, o_b, o_c = token_mixers(pa, pb, pc, bg, bias, w_gla_gate[l], b_gla_gate[l], gla_norm[l],
                                     attn_sinks[l], b, s)

        merged = merge(o_a, o_b, o_c, gates, w_br_a[l].astype(BF16), w_br_b[l].astype(BF16),
                       w_br_c[l].astype(BF16))
        y = matmul(merged, w_out[l].astype(BF16), out_dtype=F32, name="proj_out")
        xf, hn = norm_residual(y, xf, g_post_mix[l], g_pre_ffn[l])

        h = ffn_up(hn, _pad_ff(_cols(w_up[l], 0, D_FF), 1), _pad_ff(_cols(w_up[l], D_FF, D_FF), 1),
                   _halves_padded(conv_w[l]), _halves_padded(conv_b[l]).reshape(1, 2 * D_FF_P), seq=s)
        f = matmul_k(h, _pad_ff(w_down[l].astype(BF16), 0), name="ffn_down")
        xf, xn = norm_residual(f, xf, g_post_ffn[l], g_pre_mix[l + 1] if l + 1 < depth else None)
    return xf.reshape(b, s, d)
```

```python
import functools
import math

import jax
import jax.numpy as jnp
from jax import lax
from jax.experimental import pallas as pl
from jax.experimental.pallas import tpu as pltpu

F32 = jnp.float32
BF16 = jnp.bfloat16

D_MODEL = 4096
HEAD_DIM = 128
BLOCK = 128
NORM_EPS = 1e-6
DIL_GROUPS = ((128, 1), (512, 4), (2048, 16))
A_HEADS_PER_GROUP = 4
A_HEADS = 12
A_OUT = 512
B_HEADS = 8
B_DK = 64
B_DV = 128
B_GATE_RANK = 16
B_GATE_TAU = 16.0
B_CHUNK = 64
B_OUT = 1024
C_Q_HEADS = 12
C_KV_HEADS = 3
C_WINDOW = 128
C_OUT = 1536
REL_BUCKETS = 32
REL_MAX_DIST = 2048
D_FF = 11008
CONV_W = 3

A_COLS = 3 * A_HEADS * HEAD_DIM
B_COLS = 2 * B_HEADS * B_DK + 2 * B_HEADS * B_DV
C_COLS = (C_Q_HEADS + 2 * C_KV_HEADS) * HEAD_DIM
G_COLS = 3 * D_MODEL
BG_PAD = 128
OFF_B = A_COLS
OFF_BG = OFF_B + B_COLS
OFF_C = OFF_BG + B_GATE_RANK
OFF_G = OFF_C + C_COLS

HALO = 16

VMEM_LIMIT = 56 * 1024 * 1024


def _cparams(sem):
    return pltpu.CompilerParams(dimension_semantics=sem, vmem_limit_bytes=VMEM_LIMIT)


def _rms(x, g):
    ms = jnp.mean(x * x, axis=-1, keepdims=True)
    return x * lax.rsqrt(ms + NORM_EPS) * g


def _rms_cast_kernel(x_ref, g_ref, o_ref):
    o_ref[...] = _rms(x_ref[...], g_ref[...]).astype(o_ref.dtype)


def rms_cast(x, g, *, tr=256):
    m, d = x.shape
    return pl.pallas_call(
        _rms_cast_kernel,
        grid=(m // tr,),
        in_specs=[pl.BlockSpec((tr, d), lambda i: (i, 0)),
                  pl.BlockSpec((1, d), lambda i: (0, 0))],
        out_specs=pl.BlockSpec((tr, d), lambda i: (i, 0)),
        out_shape=jax.ShapeDtypeStruct((m, d), BF16),
        compiler_params=_cparams(("parallel",)),
        name="rms_cast",
    )(x, g.reshape(1, d))


def _norm_res_kernel(y_ref, x_ref, g_ref, g2_ref, xo_ref, hn_ref):
    xn = x_ref[...] + _rms(y_ref[...].astype(F32), g_ref[...])
    xo_ref[...] = xn
    hn_ref[...] = _rms(xn, g2_ref[...]).astype(hn_ref.dtype)


def _norm_res_last_kernel(y_ref, x_ref, g_ref, xo_ref):
    xo_ref[...] = x_ref[...] + _rms(y_ref[...].astype(F32), g_ref[...])


def norm_residual(y, x, g, g_next=None, *, tr=256):
    m, d = x.shape
    row = pl.BlockSpec((tr, d), lambda i: (i, 0))
    vec = pl.BlockSpec((1, d), lambda i: (0, 0))
    if g_next is None:
        return pl.pallas_call(
            _norm_res_last_kernel, grid=(m // tr,),
            in_specs=[row, row, vec], out_specs=row,
            out_shape=jax.ShapeDtypeStruct((m, d), F32),
            compiler_params=_cparams(("parallel",)), name="norm_res_last",
        )(y, x, g.reshape(1, d)), None
    return pl.pallas_call(
        _norm_res_kernel, grid=(m // tr,),
        in_specs=[row, row, vec, vec], out_specs=[row, row],
        out_shape=[jax.ShapeDtypeStruct((m, d), F32), jax.ShapeDtypeStruct((m, d), BF16)],
        compiler_params=_cparams(("parallel",)), name="norm_res",
    )(y, x, g.reshape(1, d), g_next.reshape(1, d))


def _mm_kernel(a_ref, w_ref, o_ref, *, act):
    acc = jnp.dot(a_ref[...], w_ref[...], preferred_element_type=F32)
    if act == "sigmoid":
        acc = jax.nn.sigmoid(acc)
    o_ref[...] = acc.astype(o_ref.dtype)


def matmul(a, w, *, tm=1024, tn=1024, out_dtype=BF16, act=None, name="mm"):
    m, k = a.shape
    n = w.shape[1]
    tm = min(tm, m)
    tn = min(tn, n)
    assert m % tm == 0 and n % tn == 0
    return pl.pallas_call(
        functools.partial(_mm_kernel, act=act),
        grid=(m // tm, n // tn),
        in_specs=[pl.BlockSpec((tm, k), lambda i, j: (i, 0)),
                  pl.BlockSpec((k, tn), lambda i, j: (0, j))],
        out_specs=pl.BlockSpec((tm, tn), lambda i, j: (i, j)),
        out_shape=jax.ShapeDtypeStruct((m, n), out_dtype),
        compiler_params=_cparams(("parallel", "arbitrary")),
        name=name,
    )(a, w)


def _mmk_kernel(a_ref, w_ref, o_ref, acc_ref):
    kk = pl.program_id(2)

    @pl.when(kk == 0)
    def _():
        acc_ref[...] = jnp.zeros_like(acc_ref)

    acc_ref[...] += jnp.dot(a_ref[...], w_ref[...], preferred_element_type=F32)

    @pl.when(kk == pl.num_programs(2) - 1)
    def _():
        o_ref[...] = acc_ref[...].astype(o_ref.dtype)


def matmul_k(a, w, *, tm=1024, tn=1024, tk=2816, out_dtype=F32, name="mmk"):
    m, k = a.shape
    n = w.shape[1]
    tm = min(tm, m)
    assert m % tm == 0 and n % tn == 0 and k % tk == 0
    return pl.pallas_call(
        _mmk_kernel,
        grid=(m // tm, n // tn, k // tk),
        in_specs=[pl.BlockSpec((tm, tk), lambda i, j, kk: (i, kk)),
                  pl.BlockSpec((tk, tn), lambda i, j, kk: (kk, j))],
        out_specs=pl.BlockSpec((tm, tn), lambda i, j, kk: (i, j)),
        out_shape=jax.ShapeDtypeStruct((m, n), out_dtype),
        scratch_shapes=[pltpu.VMEM((tm, tn), F32)],
        compiler_params=_cparams(("parallel", "arbitrary", "arbitrary")),
        name=name,
    )(a, w)


def _merge_kernel(oa_ref, ob_ref, oc_ref, ga_ref, gb_ref, gc_ref, wa_ref, wb_ref, wc_ref, o_ref):
    acc = ga_ref[...].astype(F32) * jnp.dot(oa_ref[...], wa_ref[...], preferred_element_type=F32)
    acc = acc + gb_ref[...].astype(F32) * jnp.dot(ob_ref[...], wb_ref[...], preferred_element_type=F32)
    acc = acc + gc_ref[...].astype(F32) * jnp.dot(oc_ref[...], wc_ref[...], preferred_element_type=F32)
    o_ref[...] = acc.astype(o_ref.dtype)


def merge(o_a, o_b, o_c, gates, wa, wb, wc, *, tm=1024, tn=1024):
    m = o_a.shape[0]
    d = wa.shape[1]
    tm = min(tm, m)
    nj = d // tn
    act = lambda kdim: pl.BlockSpec((tm, kdim), lambda i, j: (i, 0))
    gate = lambda s: pl.BlockSpec((tm, tn), lambda i, j: (i, s * nj + j))
    wgt = lambda kdim: pl.BlockSpec((kdim, tn), lambda i, j: (0, j))
    return pl.pallas_call(
        _merge_kernel,
        grid=(m // tm, nj),
        in_specs=[act(o_a.shape[1]), act(o_b.shape[1]), act(o_c.shape[1]),
                  gate(0), gate(1), gate(2),
                  wgt(wa.shape[0]), wgt(wb.shape[0]), wgt(wc.shape[0])],
        out_specs=pl.BlockSpec((tm, tn), lambda i, j: (i, j)),
        out_shape=jax.ShapeDtypeStruct((m, d), BF16),
        compiler_params=_cparams(("parallel", "arbitrary")),
        name="merge",
    )(o_a, o_b, o_c, gates, gates, gates, wa, wb, wc)


def _ffn_up_kernel(hn_ref, halo_ref, wg_ref, wu_ref, cwg_ref, cwu_ref, cbg_ref, cbu_ref, o_ref, *, tiles_per_seq):
    tm = hn_ref.shape[0]
    first = (pl.program_id(0) % tiles_per_seq) == 0
    halo = halo_ref[...]
    halo = jnp.where(first, jnp.zeros_like(halo), halo)
    a = jnp.concatenate([halo, hn_ref[...]], axis=0)

    def conv(w_ref, cw_ref, cb_ref):
        u = jnp.dot(a, w_ref[...], preferred_element_type=F32)
        cw = cw_ref[...]
        y = cb_ref[...] + cw[0:1] * u[HALO - 2:HALO - 2 + tm]
        y = y + cw[1:2] * u[HALO - 1:HALO - 1 + tm]
        return y + cw[2:3] * u[HALO:HALO + tm]

    gate = conv(wg_ref, cwg_ref, cbg_ref)
    up = conv(wu_ref, cwu_ref, cbu_ref)
    o_ref[...] = (jax.nn.silu(gate) * up).astype(o_ref.dtype)


def ffn_up(hn, w_gate, w_upper, cw_gate, cw_upper, cb_gate, cb_upper, *, seq, tm=1024, tn=512):
    m, d = hn.shape
    ff = w_gate.shape[1]
    tm = min(tm, seq)
    assert seq % tm == 0 and tm % HALO == 0
    hb = tm // HALO
    col = lambda rows: pl.BlockSpec((rows, tn), lambda i, j: (0, j))
    return pl.pallas_call(
        functools.partial(_ffn_up_kernel, tiles_per_seq=seq // tm),
        grid=(m // tm, pl.cdiv(ff, tn)),
        in_specs=[pl.BlockSpec((tm, d), lambda i, j: (i, 0)),
                  pl.BlockSpec((HALO, d), lambda i, j: (jnp.maximum(i * hb - 1, 0), 0)),
                  col(d), col(d), col(CONV_W), col(CONV_W), col(1), col(1)],
        out_specs=pl.BlockSpec((tm, tn), lambda i, j: (i, j)),
        out_shape=jax.ShapeDtypeStruct((m, ff), BF16),
        compiler_params=_cparams(("parallel", "arbitrary")),
        name="ffn_up",
    )(hn, hn, w_gate, w_upper, cw_gate, cw_upper, cb_gate, cb_upper)


NEG = -1e30
N_BIAS_HEADS = A_HEADS + C_Q_HEADS


def _bucket_tiles():
    qi = jnp.arange(BLOCK)[:, None]
    kj = jnp.arange(2 * BLOCK)[None, :]
    rel = qi + BLOCK - kj
    max_exact = REL_BUCKETS // 2
    tiles = []
    for dilation, span in [(d, w // d) for w, d in DIL_GROUPS] + [(1, C_WINDOW - 1)]:
        dist = jnp.maximum(rel * dilation, 0)
        far = max_exact + (jnp.log(jnp.maximum(dist, 1).astype(F32) / max_exact)
                           / math.log(REL_MAX_DIST / max_exact) * (REL_BUCKETS - max_exact)).astype(jnp.int32)
        bucket = jnp.where(dist < max_exact, dist, jnp.minimum(far, REL_BUCKETS - 1))
        tiles.append(jnp.where((rel >= 0) & (rel <= span), bucket, -1))
    return jnp.stack(tiles).astype(jnp.int32)


def _bias_kernel(tab_ref, idx_ref, o_ref):
    h = pl.program_id(0)
    idx = idx_ref[...]
    acc = jnp.full(idx.shape, NEG, F32)
    for bkt in range(REL_BUCKETS):
        acc = jnp.where(idx == bkt, tab_ref[bkt, h], acc)
    col = lax.broadcasted_iota(jnp.int32, idx.shape, 1)
    o_ref[0] = jnp.where(col < BLOCK, NEG, acc)
    o_ref[1] = acc


def bias_tiles(rel_bias):
    groups = len(DIL_GROUPS)
    return pl.pallas_call(
        _bias_kernel,
        grid=(N_BIAS_HEADS,),
        in_specs=[pl.BlockSpec(memory_space=pltpu.SMEM),
                  pl.BlockSpec((None, BLOCK, 2 * BLOCK),
                               lambda h: (jnp.minimum(h // A_HEADS_PER_GROUP, groups), 0, 0))],
        out_specs=pl.BlockSpec((2, None, BLOCK, 2 * BLOCK), lambda h: (0, h, 0, 0)),
        out_shape=jax.ShapeDtypeStruct((2, N_BIAS_HEADS, BLOCK, 2 * BLOCK), F32),
        compiler_params=_cparams(("arbitrary",)),
        name="bias_tiles",
    )(rel_bias, _bucket_tiles())


ATT_SCALE = HEAD_DIM ** -0.5
HEADS_PER_STEP = 4
STEP_COLS = HEADS_PER_STEP * HEAD_DIM


def _scores(q, kp, kc, bias):
    k = jnp.concatenate([kp, kc], axis=0)
    s = lax.dot_general(q, k, (((1,), (1,)), ((), ())), preferred_element_type=F32)
    return s * ATT_SCALE + bias


def _attn_a_kernel(q_ref, kc_ref, kp_ref, vc_ref, vp_ref, bias_ref, o_ref, lse_ref, *, dilation, heads):
    for r in range(dilation):
        rows = pl.ds(r, BLOCK, stride=dilation) if dilation > 1 else slice(None)
        for h in range(heads):
            sl = slice(h * HEAD_DIM, (h + 1) * HEAD_DIM)
            ld = lambda ref: ref[rows, sl].astype(BF16)
            s = _scores(ld(q_ref), ld(kp_ref), ld(kc_ref), bias_ref[h])
            m = jnp.max(s, axis=-1, keepdims=True)
            p = jnp.exp(s - m)
            l = jnp.sum(p, axis=-1, keepdims=True)
            v = jnp.concatenate([ld(vp_ref), ld(vc_ref)], axis=0)
            o = jnp.dot(p.astype(BF16), v, preferred_element_type=F32)
            o_ref[rows, sl] = o / l
            lse_ref[rows, sl] = jnp.broadcast_to(m + jnp.log(l), (BLOCK, HEAD_DIM))


def attn_dilated_group(pg, bias, gi, dilation, b, s):
    rows = BLOCK * dilation
    nb = s // rows
    heads = HEADS_PER_STEP if dilation == 1 else 1
    cols = heads * HEAD_DIM
    hs = HEADS_PER_STEP // heads
    pgv = pg.reshape(b, s, 3 * STEP_COLS)

    def blk(sec, back):
        return pl.BlockSpec((None, rows, cols), lambda bi, n, h: (bi, jnp.maximum(n - back, 0), sec * hs + h))

    out_spec = pl.BlockSpec((None, rows, cols), lambda bi, n, h: (bi, n, h))
    out_sds = jax.ShapeDtypeStruct((b, s, STEP_COLS), F32)
    o, lse = pl.pallas_call(
        functools.partial(_attn_a_kernel, dilation=dilation, heads=heads),
        grid=(b, nb, hs),
        in_specs=[blk(0, 0), blk(1, 0), blk(1, 1), blk(2, 0), blk(2, 1),
                  pl.BlockSpec((None, heads, BLOCK, 2 * BLOCK),
                               lambda bi, n, h: (jnp.minimum(n, 1), gi * hs + h, 0, 0))],
        out_specs=[out_spec, out_spec],
        out_shape=[out_sds, out_sds],
        compiler_params=_cparams(("parallel", "arbitrary", "arbitrary")),
        name=f"attn_dil{dilation}",
    )(pgv, pgv, pgv, pgv, pgv, bias)
    return o.reshape(b * s, STEP_COLS), lse.reshape(b * s, STEP_COLS)


def _lse_mix_kernel(o0_ref, o1_ref, o2_ref, l0_ref, l1_ref, l2_ref, o_ref):
    l0, l1, l2 = l0_ref[...], l1_ref[...], l2_ref[...]
    mx = jnp.maximum(jnp.maximum(l0, l1), l2)
    e0, e1, e2 = jnp.exp(l0 - mx), jnp.exp(l1 - mx), jnp.exp(l2 - mx)
    num = e0 * o0_ref[...] + e1 * o1_ref[...] + e2 * o2_ref[...]
    o_ref[...] = (num / (e0 + e1 + e2)).astype(o_ref.dtype)


def lse_mix(outs, lses, *, tr=1024):
    m, c = outs[0].shape
    tr = min(tr, m)
    spec = pl.BlockSpec((tr, c), lambda i: (i, 0))
    return pl.pallas_call(
        _lse_mix_kernel, grid=(m // tr,),
        in_specs=[spec] * 6, out_specs=spec,
        out_shape=jax.ShapeDtypeStruct((m, c), BF16),
        compiler_params=_cparams(("parallel",)), name="lse_mix",
    )(*outs, *lses)


def _attn_c_kernel(sink_ref, q_ref, kc_ref, kp_ref, vc_ref, vp_ref, bias_ref, o_ref):
    kh = pl.program_id(2)
    v = jnp.concatenate([vp_ref[...], vc_ref[...]], axis=0)
    for gq in range(HEADS_PER_STEP):
        sl = slice(gq * HEAD_DIM, (gq + 1) * HEAD_DIM)
        s = _scores(q_ref[:, sl], kp_ref[...], kc_ref[...], bias_ref[gq])
        sink = sink_ref[kh * HEADS_PER_STEP + gq]
        m = jnp.maximum(jnp.max(s, axis=-1, keepdims=True), sink)
        w = jnp.exp(s - m)
        den = jnp.sum(w, axis=-1, keepdims=True) + jnp.exp(sink - m)
        o = jnp.dot(w.astype(BF16), v, preferred_element_type=F32)
        o_ref[:, sl] = (o / den).astype(o_ref.dtype)


def attn_swa(pc, bias, sinks, b, s):
    nb = s // BLOCK
    pcv = pc.reshape(b, s, C_COLS)
    koff = C_Q_HEADS
    voff = C_Q_HEADS + C_KV_HEADS

    def kv(off, back):
        return pl.BlockSpec((None, BLOCK, HEAD_DIM),
                            lambda bi, n, kh: (bi, jnp.maximum(n - back, 0), off + kh))

    o = pl.pallas_call(
        _attn_c_kernel,
        grid=(b, nb, C_KV_HEADS),
        in_specs=[pl.BlockSpec(memory_space=pltpu.SMEM),
                  pl.BlockSpec((None, BLOCK, STEP_COLS), lambda bi, n, kh: (bi, n, kh)),
                  kv(koff, 0), kv(koff, 1), kv(voff, 0), kv(voff, 1),
                  pl.BlockSpec((None, HEADS_PER_STEP, BLOCK, 2 * BLOCK),
                               lambda bi, n, kh: (jnp.minimum(n, 1), A_HEADS // HEADS_PER_STEP + kh, 0, 0))],
        out_specs=pl.BlockSpec((None, BLOCK, STEP_COLS), lambda bi, n, kh: (bi, n, kh)),
        out_shape=jax.ShapeDtypeStruct((b, s, C_OUT), BF16),
        compiler_params=_cparams(("parallel", "parallel", "arbitrary")),
        name="attn_swa",
    )(sinks, pcv, pcv, pcv, pcv, pcv, bias)
    return o.reshape(b * s, C_OUT)


B_QK = B_HEADS * B_DK


def _split3(x):
    hi = x.astype(BF16)
    r1 = x - hi.astype(F32)
    mid = r1.astype(BF16)
    lo = (r1 - mid.astype(F32)).astype(BF16)
    return hi, mid, lo


def _gla_kernel(q_ref, k_ref, v_ref, r_ref, bg_ref, wg_ref, bgate_ref, gnorm_ref, o_ref, state_ref):
    @pl.when(pl.program_id(1) == 0)
    def _():
        state_ref[...] = jnp.zeros_like(state_ref)

    z = jnp.dot(bg_ref[...].astype(BF16), wg_ref[...], preferred_element_type=F32) + bgate_ref[...]
    log_a = (jnp.minimum(z, 0.0) - jnp.log(1.0 + jnp.exp(-jnp.abs(z)))) / B_GATE_TAU

    row = lax.broadcasted_iota(jnp.int32, (B_CHUNK, B_CHUNK), 0)
    coli = lax.broadcasted_iota(jnp.int32, (B_CHUNK, B_CHUNK), 1)
    causal = row >= coli
    tril = jnp.where(causal, 1.0, 0.0).astype(BF16)
    cum = sum(jnp.dot(tril, part, preferred_element_type=F32) for part in _split3(log_a))
    last = cum[B_CHUNK - 1:B_CHUNK, :]

    qf = q_ref[...].astype(F32) * (B_DK ** -0.5)
    kf = k_ref[...].astype(F32)
    q_dec = (qf * jnp.exp(cum)).astype(BF16)
    k_inv = (kf * jnp.exp(-cum)).astype(BF16)
    k_out = (kf * jnp.exp(last - cum)).astype(BF16)
    decay = jnp.exp(last)

    for h in range(B_HEADS):
        ks = slice(h * B_DK, (h + 1) * B_DK)
        vs = slice(h * B_DV, (h + 1) * B_DV)
        qd, ki, ko, vh = q_dec[:, ks], k_inv[:, ks], k_out[:, ks], v_ref[:, vs]
        att = lax.dot_general(qd, ki, (((1,), (1,)), ((), ())), preferred_element_type=F32)
        att = jnp.where(causal, att, 0.0).astype(BF16)
        st = state_ref[h]
        o = jnp.dot(att, vh, preferred_element_type=F32)
        o = o + lax.dot_general(qd, st.astype(BF16), (((1,), (1,)), ((), ())), preferred_element_type=F32)
        upd = lax.dot_general(vh, ko, (((0,), (0,)), ((), ())), preferred_element_type=F32)
        state_ref[h] = st * decay[:, ks] + upd
        on = _rms(o, gnorm_ref[:, vs])
        o_ref[:, vs] = (on * jax.nn.silu(r_ref[:, vs].astype(F32))).astype(o_ref.dtype)


def gla(pb, bg, w_gate, b_gate, gnorm, b, s):
    nc = s // B_CHUNK
    pbv = pb.reshape(b, s, B_COLS)
    bgv = bg.reshape(b, s, BG_PAD)
    wg = jnp.pad(w_gate, ((0, BG_PAD - B_GATE_RANK), (0, 0))).astype(BF16)
    full = lambda shape: pl.BlockSpec(shape, lambda bi, c: (0, 0))
    o = pl.pallas_call(
        _gla_kernel,
        grid=(b, nc),
        in_specs=[pl.BlockSpec((None, B_CHUNK, B_QK), lambda bi, c: (bi, c, 0)),
                  pl.BlockSpec((None, B_CHUNK, B_QK), lambda bi, c: (bi, c, 1)),
                  pl.BlockSpec((None, B_CHUNK, B_OUT), lambda bi, c: (bi, c, 1)),
                  pl.BlockSpec((None, B_CHUNK, B_OUT), lambda bi, c: (bi, c, 2)),
                  pl.BlockSpec((None, B_CHUNK, BG_PAD), lambda bi, c: (bi, c, 0)),
                  full((BG_PAD, B_QK)), full((1, B_QK)), full((1, B_OUT))],
        out_specs=pl.BlockSpec((None, B_CHUNK, B_OUT), lambda bi, c: (bi, c, 0)),
        out_shape=jax.ShapeDtypeStruct((b, s, B_OUT), BF16),
        scratch_shapes=[pltpu.VMEM((B_HEADS, B_DV, B_DK), F32)],
        compiler_params=_cparams(("parallel", "arbitrary")),
        name="gla",
    )(pbv, pbv, pbv, pbv, bgv, wg, b_gate.reshape(1, B_QK), jnp.tile(gnorm, B_HEADS).reshape(1, B_OUT))
    return o.reshape(b * s, B_OUT)


def token_mixers(pgs, pb, pc, bg, bias, w_gla_gate, b_gla_gate, gla_norm, attn_sinks, b, s):
    outs, lses = [], []
    for gi, (_, dilation) in enumerate(DIL_GROUPS):
        o, lse = attn_dilated_group(pgs[gi], bias, gi, dilation, b, s)
        outs.append(o)
        lses.append(lse)
    o_a = lse_mix(outs, lses)
    o_b = gla(pb, bg, w_gla_gate, b_gla_gate, gla_norm, b, s)
    o_c = attn_swa(pc, bias, attn_sinks, b, s)
    return o_a, o_b, o_c


def _cast_kernel(w_ref, o_ref):
    o_ref[...] = w_ref[...].astype(o_ref.dtype)


def cast_layer(w, l, *, tr=256):
    _, rows, cols = w.shape
    return pl.pallas_call(
        _cast_kernel, grid=(rows // tr,),
        in_specs=[pl.BlockSpec((None, tr, cols), lambda i: (l, i, 0))],
        out_specs=pl.BlockSpec((tr, cols), lambda i: (i, 0)),
        out_shape=jax.ShapeDtypeStruct((rows, cols), BF16),
        compiler_params=_cparams(("parallel",)), name="cast_w",
    )(w)


def _cast_halves_kernel(w_ref, lo_ref, hi_ref):
    half = lo_ref.shape[1]
    lo_ref[...] = w_ref[:, :half].astype(lo_ref.dtype)
    hi_ref[...] = w_ref[:, half:].astype(hi_ref.dtype)


def cast_halves(w, l, *, tr=64):
    _, rows, cols = w.shape
    half = cols // 2
    out = pl.BlockSpec((tr, half), lambda i: (i, 0))
    sds = jax.ShapeDtypeStruct((rows, half), BF16)
    return pl.pallas_call(
        _cast_halves_kernel, grid=(rows // tr,),
        in_specs=[pl.BlockSpec((None, tr, cols), lambda i: (l, i, 0))],
        out_specs=[out, out], out_shape=[sds, sds],
        compiler_params=_cparams(("parallel",)), name="cast_halves",
    )(w)


A_SEC = A_HEADS * HEAD_DIM


def _cast_w_in_kernel(w_ref, a0_ref, a1_ref, a2_ref, b_ref, bg_ref, c_ref, g_ref):
    cvt = lambda lo, n: w_ref[:, lo:lo + n].astype(BF16)
    for gi, a_ref in enumerate((a0_ref, a1_ref, a2_ref)):
        for sec in range(3):
            a_ref[:, sec * STEP_COLS:(sec + 1) * STEP_COLS] = cvt(sec * A_SEC + gi * STEP_COLS, STEP_COLS)
    b_ref[...] = cvt(OFF_B, B_COLS)
    bg_ref[...] = jnp.zeros(bg_ref.shape, BF16)
    bg_ref[:, :B_GATE_RANK] = cvt(OFF_BG, B_GATE_RANK)
    c_ref[...] = cvt(OFF_C, C_COLS)
    g_ref[...] = cvt(OFF_G, G_COLS)


def cast_w_in(w_in, l, *, tr=64):
    _, rows, cols = w_in.shape
    widths = (3 * STEP_COLS,) * len(DIL_GROUPS) + (B_COLS, BG_PAD, C_COLS, G_COLS)
    return pl.pallas_call(
        _cast_w_in_kernel, grid=(rows // tr,),
        in_specs=[pl.BlockSpec((None, tr, cols), lambda i: (l, i, 0))],
        out_specs=[pl.BlockSpec((tr, n), lambda i: (i, 0)) for n in widths],
        out_shape=[jax.ShapeDtypeStruct((rows, n), BF16) for n in widths],
        compiler_params=_cparams(("parallel",)), name="cast_w_in",
    )(w_in)


def kernel(x, rel_bias, w_in, w_gla_gate, b_gla_gate, gla_norm, attn_sinks, w_br_a, w_br_b, w_br_c, w_out,
           g_pre_mix, g_post_mix, g_pre_ffn, g_post_ffn, w_up, conv_w, conv_b, w_down):
    b, s, d = x.shape
    depth = w_in.shape[0]
    xf = x.reshape(b * s, d)
    xn = rms_cast(xf, g_pre_mix[0])
    bias = bias_tiles(rel_bias)
    for l in range(depth):
        *w_a, w_b, w_bg, w_c, w_g = cast_w_in(w_in, l)
        pgs = [matmul(xn, w_a[gi], tn=768, out_dtype=BF16 if dil == 1 else F32, name=f"proj_a{gi}")
               for gi, (_, dil) in enumerate(DIL_GROUPS)]
        pb = matmul(xn, w_b, tn=768, name="proj_b")
        pc = matmul(xn, w_c, tn=768, name="proj_c")
        gates = matmul(xn, w_g, tn=768, act="sigmoid", name="proj_g")
        bg = matmul(xn, w_bg, out_dtype=F32, name="proj_bg")

        o_a, o_b, o_c = token_mixers(pgs, pb, pc, bg, bias, w_gla_gate[l], b_gla_gate[l], gla_norm[l],
                                     attn_sinks[l], b, s)

        merged = merge(o_a, o_b, o_c, gates, cast_layer(w_br_a, l), cast_layer(w_br_b, l), cast_layer(w_br_c, l))
        y = matmul(merged, cast_layer(w_out, l), out_dtype=F32, name="proj_out")
        xf, hn = norm_residual(y, xf, g_post_mix[l], g_pre_ffn[l])

        w_gate, w_upper = cast_halves(w_up, l)
        h = ffn_up(hn, w_gate, w_upper, conv_w[l][:, :D_FF], conv_w[l][:, D_FF:],
                   conv_b[l][:D_FF].reshape(1, D_FF), conv_b[l][D_FF:].reshape(1, D_FF), seq=s)
        f = matmul_k(h, cast_layer(w_down, l), tn=512, tk=D_FF // 2, name="ffn_down")
        xf, xn = norm_residual(f, xf, g_post_ffn[l], g_pre_mix[l + 1] if l + 1 < depth else None)
    return xf.reshape(b, s, d)
```

```python
import functools
import math

import jax
import jax.numpy as jnp
from jax import lax
from jax.experimental import pallas as pl
from jax.experimental.pallas import tpu as pltpu

F32 = jnp.float32
BF16 = jnp.bfloat16

D_MODEL = 4096
HEAD_DIM = 128
BLOCK = 128
NORM_EPS = 1e-6
DIL_GROUPS = ((128, 1), (512, 4), (2048, 16))
A_HEADS_PER_GROUP = 4
A_HEADS = 12
A_OUT = 512
B_HEADS = 8
B_DK = 64
B_DV = 128
B_GATE_RANK = 16
B_GATE_TAU = 16.0
B_CHUNK = 64
B_OUT = 1024
C_Q_HEADS = 12
C_KV_HEADS = 3
C_WINDOW = 128
C_OUT = 1536
REL_BUCKETS = 32
REL_MAX_DIST = 2048
D_FF = 11008
CONV_W = 3

A_COLS = 3 * A_HEADS * HEAD_DIM
B_COLS = 2 * B_HEADS * B_DK + 2 * B_HEADS * B_DV
C_COLS = (C_Q_HEADS + 2 * C_KV_HEADS) * HEAD_DIM
G_COLS = 3 * D_MODEL
OFF_B = A_COLS
OFF_BG = OFF_B + B_COLS
OFF_C = OFF_BG + B_GATE_RANK
OFF_G = OFF_C + C_COLS


VMEM_LIMIT = 56 * 1024 * 1024


def _cparams(sem):
    return pltpu.CompilerParams(dimension_semantics=sem, vmem_limit_bytes=VMEM_LIMIT)


def _rms(x, g):
    ms = jnp.mean(x * x, axis=-1, keepdims=True)
    return x * lax.rsqrt(ms + NORM_EPS) * g


def _rms_cast_kernel(x_ref, g_ref, o_ref):
    o_ref[...] = _rms(x_ref[...], g_ref[...]).astype(o_ref.dtype)


def rms_cast(x, g, *, tr=256):
    m, d = x.shape
    return pl.pallas_call(
        _rms_cast_kernel,
        grid=(m // tr,),
        in_specs=[pl.BlockSpec((tr, d), lambda i: (i, 0)),
                  pl.BlockSpec((1, d), lambda i: (0, 0))],
        out_specs=pl.BlockSpec((tr, d), lambda i: (i, 0)),
        out_shape=jax.ShapeDtypeStruct((m, d), BF16),
        compiler_params=_cparams(("parallel",)),
        name="rms_cast",
    )(x, g.reshape(1, d))


def _norm_res_kernel(y_ref, x_ref, g_ref, g2_ref, xo_ref, hn_ref):
    xn = x_ref[...] + _rms(y_ref[...].astype(F32), g_ref[...])
    xo_ref[...] = xn
    hn_ref[...] = _rms(xn, g2_ref[...]).astype(hn_ref.dtype)


def _norm_res_last_kernel(y_ref, x_ref, g_ref, xo_ref):
    xo_ref[...] = x_ref[...] + _rms(y_ref[...].astype(F32), g_ref[...])


def norm_residual(y, x, g, g_next=None, *, tr=256):
    m, d = x.shape
    row = pl.BlockSpec((tr, d), lambda i: (i, 0))
    vec = pl.BlockSpec((1, d), lambda i: (0, 0))
    if g_next is None:
        return pl.pallas_call(
            _norm_res_last_kernel, grid=(m // tr,),
            in_specs=[row, row, vec], out_specs=row,
            out_shape=jax.ShapeDtypeStruct((m, d), F32),
            compiler_params=_cparams(("parallel",)), name="norm_res_last",
        )(y, x, g.reshape(1, d)), None
    return pl.pallas_call(
        _norm_res_kernel, grid=(m // tr,),
        in_specs=[row, row, vec, vec], out_specs=[row, row],
        out_shape=[jax.ShapeDtypeStruct((m, d), F32), jax.ShapeDtypeStruct((m, d), BF16)],
        compiler_params=_cparams(("parallel",)), name="norm_res",
    )(y, x, g.reshape(1, d), g_next.reshape(1, d))


def _mm_kernel(a_ref, w_ref, o_ref, *, act, trans_b):
    contract = (((1,), (1 if trans_b else 0,)), ((), ()))
    acc = lax.dot_general(a_ref[...], w_ref[...], contract, preferred_element_type=F32)
    if act == "sigmoid":
        acc = jax.nn.sigmoid(acc)
    o_ref[...] = acc.astype(o_ref.dtype)


def matmul(a, w, *, trans_b=False, tm=1024, tn=1024, out_dtype=BF16, act=None, name="mm"):
    m, k = a.shape
    n = w.shape[0 if trans_b else 1]
    tm = min(tm, m)
    tn = min(tn, n)
    assert m % tm == 0 and n % tn == 0
    w_spec = pl.BlockSpec((tn, k), lambda i, j: (j, 0)) if trans_b else pl.BlockSpec((k, tn), lambda i, j: (0, j))
    return pl.pallas_call(
        functools.partial(_mm_kernel, act=act, trans_b=trans_b),
        grid=(m // tm, n // tn),
        in_specs=[pl.BlockSpec((tm, k), lambda i, j: (i, 0)), w_spec],
        out_specs=pl.BlockSpec((tm, tn), lambda i, j: (i, j)),
        out_shape=jax.ShapeDtypeStruct((m, n), out_dtype),
        compiler_params=_cparams(("parallel", "arbitrary")),
        name=name,
    )(a, w)


def _mmk_kernel(a_ref, w_ref, o_ref, acc_ref):
    kk = pl.program_id(2)

    @pl.when(kk == 0)
    def _():
        acc_ref[...] = jnp.zeros_like(acc_ref)

    acc_ref[...] += jnp.dot(a_ref[...], w_ref[...], preferred_element_type=F32)

    @pl.when(kk == pl.num_programs(2) - 1)
    def _():
        o_ref[...] = acc_ref[...].astype(o_ref.dtype)


def matmul_k(a, w, *, tm=1024, tn=1024, tk=2816, out_dtype=F32, name="mmk"):
    m, k = a.shape
    n = w.shape[1]
    tm = min(tm, m)
    assert m % tm == 0 and n % tn == 0 and k % tk == 0
    return pl.pallas_call(
        _mmk_kernel,
        grid=(m // tm, n // tn, k // tk),
        in_specs=[pl.BlockSpec((tm, tk), lambda i, j, kk: (i, kk)),
                  pl.BlockSpec((tk, tn), lambda i, j, kk: (kk, j))],
        out_specs=pl.BlockSpec((tm, tn), lambda i, j, kk: (i, j)),
        out_shape=jax.ShapeDtypeStruct((m, n), out_dtype),
        scratch_shapes=[pltpu.VMEM((tm, tn), F32)],
        compiler_params=_cparams(("parallel", "arbitrary", "arbitrary")),
        name=name,
    )(a, w)


def _merge_kernel(oa_ref, ob_ref, oc_ref, ga_ref, gb_ref, gc_ref, wa_ref, wb_ref, wc_ref, o_ref):
    acc = ga_ref[...].astype(F32) * jnp.dot(oa_ref[...], wa_ref[...], preferred_element_type=F32)
    acc = acc + gb_ref[...].astype(F32) * jnp.dot(ob_ref[...], wb_ref[...], preferred_element_type=F32)
    acc = acc + gc_ref[...].astype(F32) * jnp.dot(oc_ref[...], wc_ref[...], preferred_element_type=F32)
    o_ref[...] = acc.astype(o_ref.dtype)


def merge(o_a, o_b, o_c, gates, wa, wb, wc, *, tm=1024, tn=1024):
    m = o_a.shape[0]
    d = wa.shape[1]
    tm = min(tm, m)
    nj = d // tn
    act = lambda kdim: pl.BlockSpec((tm, kdim), lambda i, j: (i, 0))
    gate = lambda s: pl.BlockSpec((tm, tn), lambda i, j: (i, s * nj + j))
    wgt = lambda kdim: pl.BlockSpec((kdim, tn), lambda i, j: (0, j))
    return pl.pallas_call(
        _merge_kernel,
        grid=(m // tm, nj),
        in_specs=[act(o_a.shape[1]), act(o_b.shape[1]), act(o_c.shape[1]),
                  gate(0), gate(1), gate(2),
                  wgt(wa.shape[0]), wgt(wb.shape[0]), wgt(wc.shape[0])],
        out_specs=pl.BlockSpec((tm, tn), lambda i, j: (i, j)),
        out_shape=jax.ShapeDtypeStruct((m, d), BF16),
        compiler_params=_cparams(("parallel", "arbitrary")),
        name="merge",
    )(o_a, o_b, o_c, gates, gates, gates, wa, wb, wc)


TAIL = 8


def _ffn_up_kernel(hn_ref, wg_ref, wu_ref, cwg_ref, cwu_ref, cbg_ref, cbu_ref, o_ref, tail_g, tail_u, *,
                   tiles_per_seq):
    tm = o_ref.shape[0]
    j = pl.program_id(1)

    @pl.when(pl.program_id(0) % tiles_per_seq == 0)
    def _():
        tail_g[j] = jnp.zeros(tail_g.shape[1:], F32)
        tail_u[j] = jnp.zeros(tail_u.shape[1:], F32)

    a = hn_ref[...]

    def conv(w_ref, cw_ref, cb_ref, tail_ref):
        u = jnp.dot(a, w_ref[...], preferred_element_type=F32)
        ext = jnp.concatenate([tail_ref[j], u], axis=0)
        tail_ref[j] = u[tm - TAIL:]
        cw = cw_ref[...]
        y = cb_ref[...] + cw[0:1] * ext[TAIL - 2:TAIL - 2 + tm]
        y = y + cw[1:2] * ext[TAIL - 1:TAIL - 1 + tm]
        return y + cw[2:3] * ext[TAIL:TAIL + tm]

    gate = conv(wg_ref, cwg_ref, cbg_ref, tail_g)
    up = conv(wu_ref, cwu_ref, cbu_ref, tail_u)
    o_ref[...] = (jax.nn.silu(gate) * up).astype(o_ref.dtype)


def ffn_up(hn, w_gate, w_upper, cw_gate, cw_upper, cb_gate, cb_upper, *, seq, tm=1024, tn=512):
    m, d = hn.shape
    ff = w_gate.shape[1]
    tm = min(tm, seq)
    assert seq % tm == 0
    nj = pl.cdiv(ff, tn)
    col = lambda rows: pl.BlockSpec((rows, tn), lambda i, j: (0, j))
    return pl.pallas_call(
        functools.partial(_ffn_up_kernel, tiles_per_seq=seq // tm),
        grid=(m // tm, nj),
        in_specs=[pl.BlockSpec((tm, d), lambda i, j: (i, 0)),
                  col(d), col(d), col(CONV_W), col(CONV_W), col(1), col(1)],
        out_specs=pl.BlockSpec((tm, tn), lambda i, j: (i, j)),
        out_shape=jax.ShapeDtypeStruct((m, ff), BF16),
        scratch_shapes=[pltpu.VMEM((nj, TAIL, tn), F32), pltpu.VMEM((nj, TAIL, tn), F32)],
        compiler_params=_cparams(("arbitrary", "arbitrary")),
        name="ffn_up",
    )(hn, w_gate, w_upper, cw_gate, cw_upper, cb_gate, cb_upper)


NEG = -1e30
N_BIAS_HEADS = A_HEADS + C_Q_HEADS


def _bucket_tiles():
    qi = jnp.arange(BLOCK)[:, None]
    kj = jnp.arange(2 * BLOCK)[None, :]
    rel = qi + BLOCK - kj
    max_exact = REL_BUCKETS // 2
    tiles = []
    for dilation, span in [(d, w // d) for w, d in DIL_GROUPS] + [(1, C_WINDOW - 1)]:
        dist = jnp.maximum(rel * dilation, 0)
        far = max_exact + (jnp.log(jnp.maximum(dist, 1).astype(F32) / max_exact)
                           / math.log(REL_MAX_DIST / max_exact) * (REL_BUCKETS - max_exact)).astype(jnp.int32)
        bucket = jnp.where(dist < max_exact, dist, jnp.minimum(far, REL_BUCKETS - 1))
        tiles.append(jnp.where((rel >= 0) & (rel <= span), bucket, -1))
    return jnp.stack(tiles).astype(jnp.int32)


def _bias_kernel(tab_ref, idx_ref, o_ref):
    h = pl.program_id(0)
    idx = idx_ref[...]
    acc = jnp.full(idx.shape, NEG, F32)
    for bkt in range(REL_BUCKETS):
        acc = jnp.where(idx == bkt, tab_ref[bkt, h], acc)
    col = lax.broadcasted_iota(jnp.int32, idx.shape, 1)
    o_ref[0] = jnp.where(col < BLOCK, NEG, acc)
    o_ref[1] = acc


def bias_tiles(rel_bias):
    groups = len(DIL_GROUPS)
    return pl.pallas_call(
        _bias_kernel,
        grid=(N_BIAS_HEADS,),
        in_specs=[pl.BlockSpec(memory_space=pltpu.SMEM),
                  pl.BlockSpec((None, BLOCK, 2 * BLOCK),
                               lambda h: (jnp.minimum(h // A_HEADS_PER_GROUP, groups), 0, 0))],
        out_specs=pl.BlockSpec((2, None, BLOCK, 2 * BLOCK), lambda h: (0, h, 0, 0)),
        out_shape=jax.ShapeDtypeStruct((2, N_BIAS_HEADS, BLOCK, 2 * BLOCK), F32),
        compiler_params=_cparams(("arbitrary",)),
        name="bias_tiles",
    )(rel_bias, _bucket_tiles())


ATT_SCALE = HEAD_DIM ** -0.5
HEADS_PER_STEP = 4
STEP_COLS = HEADS_PER_STEP * HEAD_DIM


def _scores(q, kp, kc, bias):
    k = jnp.concatenate([kp, kc], axis=0)
    s = lax.dot_general(q, k, (((1,), (1,)), ((), ())), preferred_element_type=F32)
    return s * ATT_SCALE + bias


def _attn_a_kernel(q_ref, kc_ref, kp_ref, vc_ref, vp_ref, bias_ref, o_ref, lse_ref, *, dilation, heads, qblocks):
    span = BLOCK * dilation
    first = jnp.minimum(pl.program_id(1), 1)
    for sb in range(qblocks):
        for r in range(dilation):
            def rows(blk):
                start = blk * span + r
                return pl.ds(start, BLOCK, stride=dilation) if dilation > 1 else pl.ds(start, BLOCK)

            for h in range(heads):
                sl = slice(h * HEAD_DIM, (h + 1) * HEAD_DIM)
                ld = lambda ref, blk: ref[rows(blk), sl].astype(BF16)
                prev = (lambda ref_p, ref_c: ld(ref_p, 0)) if sb == 0 else (lambda ref_p, ref_c: ld(ref_c, sb - 1))
                bias = bias_ref[first, h] if sb == 0 else bias_ref[1, h]
                s = _scores(ld(q_ref, sb), prev(kp_ref, kc_ref), ld(kc_ref, sb), bias)
                m = jnp.max(s, axis=-1, keepdims=True)
                p = jnp.exp(s - m)
                l = jnp.sum(p, axis=-1, keepdims=True)
                v = jnp.concatenate([prev(vp_ref, vc_ref), ld(vc_ref, sb)], axis=0)
                o = jnp.dot(p.astype(BF16), v, preferred_element_type=F32)
                o_ref[rows(sb), sl] = o / l
                lse_ref[rows(sb), sl] = jnp.broadcast_to(m + jnp.log(l), (BLOCK, HEAD_DIM))


def attn_dilated_group(pg, bias, gi, dilation, b, s):
    span = BLOCK * dilation
    qblocks = 4 if dilation < 16 else 1
    rows = span * qblocks
    heads = HEADS_PER_STEP if dilation == 1 else 1
    cols = heads * HEAD_DIM
    hs = HEADS_PER_STEP // heads
    pgv = pg.reshape(b, s, 3 * STEP_COLS)

    def cur(sec):
        return pl.BlockSpec((None, rows, cols), lambda bi, n, h: (bi, n, sec * hs + h))

    def prev(sec):
        return pl.BlockSpec((None, span, cols), lambda bi, n, h: (bi, jnp.maximum(n * qblocks - 1, 0), sec * hs + h))

    out_spec = pl.BlockSpec((None, rows, cols), lambda bi, n, h: (bi, n, h))
    out_sds = jax.ShapeDtypeStruct((b, s, STEP_COLS), F32)
    o, lse = pl.pallas_call(
        functools.partial(_attn_a_kernel, dilation=dilation, heads=heads, qblocks=qblocks),
        grid=(b, s // rows, hs),
        in_specs=[cur(0), cur(1), prev(1), cur(2), prev(2),
                  pl.BlockSpec((2, heads, BLOCK, 2 * BLOCK), lambda bi, n, h: (0, gi * hs + h, 0, 0))],
        out_specs=[out_spec, out_spec],
        out_shape=[out_sds, out_sds],
        compiler_params=_cparams(("parallel", "arbitrary", "arbitrary")),
        name=f"attn_dil{dilation}",
    )(pgv, pgv, pgv, pgv, pgv, bias)
    return o.reshape(b * s, STEP_COLS), lse.reshape(b * s, STEP_COLS)


def _lse_mix_kernel(o0_ref, o1_ref, o2_ref, l0_ref, l1_ref, l2_ref, o_ref):
    l0, l1, l2 = l0_ref[...], l1_ref[...], l2_ref[...]
    mx = jnp.maximum(jnp.maximum(l0, l1), l2)
    e0, e1, e2 = jnp.exp(l0 - mx), jnp.exp(l1 - mx), jnp.exp(l2 - mx)
    num = e0 * o0_ref[...] + e1 * o1_ref[...] + e2 * o2_ref[...]
    o_ref[...] = (num / (e0 + e1 + e2)).astype(o_ref.dtype)


def lse_mix(outs, lses, *, tr=1024):
    m, c = outs[0].shape
    tr = min(tr, m)
    spec = pl.BlockSpec((tr, c), lambda i: (i, 0))
    return pl.pallas_call(
        _lse_mix_kernel, grid=(m // tr,),
        in_specs=[spec] * 6, out_specs=spec,
        out_shape=jax.ShapeDtypeStruct((m, c), BF16),
        compiler_params=_cparams(("parallel",)), name="lse_mix",
    )(*outs, *lses)


SWA_QBLOCKS = 4


def _attn_c_kernel(sink_ref, q_ref, kc_ref, kp_ref, vc_ref, vp_ref, bias_ref, o_ref):
    kh = pl.program_id(2)
    first = jnp.minimum(pl.program_id(1), 1)
    for sb in range(SWA_QBLOCKS):
        rows = pl.ds(sb * BLOCK, BLOCK)
        before = pl.ds((sb - 1) * BLOCK, BLOCK)
        kprev, vprev = (kp_ref[...], vp_ref[...]) if sb == 0 else (kc_ref[before, :], vc_ref[before, :])
        v = jnp.concatenate([vprev, vc_ref[rows, :]], axis=0)
        for gq in range(HEADS_PER_STEP):
            sl = slice(gq * HEAD_DIM, (gq + 1) * HEAD_DIM)
            bias = bias_ref[first, gq] if sb == 0 else bias_ref[1, gq]
            s = _scores(q_ref[rows, sl], kprev, kc_ref[rows, :], bias)
            sink = sink_ref[kh * HEADS_PER_STEP + gq]
            m = jnp.maximum(jnp.max(s, axis=-1, keepdims=True), sink)
            w = jnp.exp(s - m)
            den = jnp.sum(w, axis=-1, keepdims=True) + jnp.exp(sink - m)
            o = jnp.dot(w.astype(BF16), v, preferred_element_type=F32)
            o_ref[rows, sl] = (o / den).astype(o_ref.dtype)


def attn_swa(pc, bias, sinks, b, s):
    rows = BLOCK * SWA_QBLOCKS
    pcv = pc.reshape(b, s, C_COLS)
    koff = C_Q_HEADS
    voff = C_Q_HEADS + C_KV_HEADS

    def cur(off):
        return pl.BlockSpec((None, rows, HEAD_DIM), lambda bi, n, kh: (bi, n, off + kh))

    def prev(off):
        return pl.BlockSpec((None, BLOCK, HEAD_DIM),
                            lambda bi, n, kh: (bi, jnp.maximum(n * SWA_QBLOCKS - 1, 0), off + kh))

    o = pl.pallas_call(
        _attn_c_kernel,
        grid=(b, s // rows, C_KV_HEADS),
        in_specs=[pl.BlockSpec(memory_space=pltpu.SMEM),
                  pl.BlockSpec((None, rows, STEP_COLS), lambda bi, n, kh: (bi, n, kh)),
                  cur(koff), prev(koff), cur(voff), prev(voff),
                  pl.BlockSpec((2, HEADS_PER_STEP, BLOCK, 2 * BLOCK),
                               lambda bi, n, kh: (0, A_HEADS // HEADS_PER_STEP + kh, 0, 0))],
        out_specs=pl.BlockSpec((None, rows, STEP_COLS), lambda bi, n, kh: (bi, n, kh)),
        out_shape=jax.ShapeDtypeStruct((b, s, C_OUT), BF16),
        compiler_params=_cparams(("parallel", "arbitrary", "arbitrary")),
        name="attn_swa",
    )(sinks, pcv, pcv, pcv, pcv, pcv, bias)
    return o.reshape(b * s, C_OUT)


B_QK = B_HEADS * B_DK


def _split3(x):
    hi = x.astype(BF16)
    r1 = x - hi.astype(F32)
    mid = r1.astype(BF16)
    lo = (r1 - mid.astype(F32)).astype(BF16)
    return hi, mid, lo


def _gla_kernel(q_ref, k_ref, v_ref, r_ref, bg_ref, wg_ref, bgate_ref, gnorm_ref, o_ref, state_ref):
    @pl.when(pl.program_id(1) == 0)
    def _():
        state_ref[...] = jnp.zeros_like(state_ref)

    z = jnp.dot(bg_ref[...].astype(BF16), wg_ref[...], preferred_element_type=F32) + bgate_ref[...]
    log_a = (jnp.minimum(z, 0.0) - jnp.log(1.0 + jnp.exp(-jnp.abs(z)))) / B_GATE_TAU

    row = lax.broadcasted_iota(jnp.int32, (B_CHUNK, B_CHUNK), 0)
    coli = lax.broadcasted_iota(jnp.int32, (B_CHUNK, B_CHUNK), 1)
    causal = row >= coli
    tril = jnp.where(causal, 1.0, 0.0).astype(BF16)
    cum = sum(jnp.dot(tril, part, preferred_element_type=F32) for part in _split3(log_a))
    last = cum[B_CHUNK - 1:B_CHUNK, :]

    qf = q_ref[...].astype(F32) * (B_DK ** -0.5)
    kf = k_ref[...].astype(F32)
    q_dec = (qf * jnp.exp(cum)).astype(BF16)
    k_inv = (kf * jnp.exp(-cum)).astype(BF16)
    k_out = (kf * jnp.exp(last - cum)).astype(BF16)
    decay = jnp.exp(last)

    for h in range(B_HEADS):
        ks = slice(h * B_DK, (h + 1) * B_DK)
        vs = slice(h * B_DV, (h + 1) * B_DV)
        qd, ki, ko, vh = q_dec[:, ks], k_inv[:, ks], k_out[:, ks], v_ref[:, vs]
        att = lax.dot_general(qd, ki, (((1,), (1,)), ((), ())), preferred_element_type=F32)
        att = jnp.where(causal, att, 0.0).astype(BF16)
        st = state_ref[h]
        o = jnp.dot(att, vh, preferred_element_type=F32)
        o = o + lax.dot_general(qd, st.astype(BF16), (((1,), (1,)), ((), ())), preferred_element_type=F32)
        upd = lax.dot_general(vh, ko, (((0,), (0,)), ((), ())), preferred_element_type=F32)
        state_ref[h] = st * decay[:, ks] + upd
        on = _rms(o, gnorm_ref[:, vs])
        o_ref[:, vs] = (on * jax.nn.silu(r_ref[:, vs].astype(F32))).astype(o_ref.dtype)


def gla(pb, bg, w_gate, b_gate, gnorm, b, s):
    nc = s // B_CHUNK
    pbv = pb.reshape(b, s, B_COLS)
    bgv = bg.reshape(b, s, B_GATE_RANK)
    wg = w_gate.astype(BF16)
    full = lambda shape: pl.BlockSpec(shape, lambda bi, c: (0, 0))
    o = pl.pallas_call(
        _gla_kernel,
        grid=(b, nc),
        in_specs=[pl.BlockSpec((None, B_CHUNK, B_QK), lambda bi, c: (bi, c, 0)),
                  pl.BlockSpec((None, B_CHUNK, B_QK), lambda bi, c: (bi, c, 1)),
                  pl.BlockSpec((None, B_CHUNK, B_OUT), lambda bi, c: (bi, c, 1)),
                  pl.BlockSpec((None, B_CHUNK, B_OUT), lambda bi, c: (bi, c, 2)),
                  pl.BlockSpec((None, B_CHUNK, B_GATE_RANK), lambda bi, c: (bi, c, 0)),
                  full((B_GATE_RANK, B_QK)), full((1, B_QK)), full((1, B_OUT))],
        out_specs=pl.BlockSpec((None, B_CHUNK, B_OUT), lambda bi, c: (bi, c, 0)),
        out_shape=jax.ShapeDtypeStruct((b, s, B_OUT), BF16),
        scratch_shapes=[pltpu.VMEM((B_HEADS, B_DV, B_DK), F32)],
        compiler_params=_cparams(("parallel", "arbitrary")),
        name="gla",
    )(pbv, pbv, pbv, pbv, bgv, wg, b_gate.reshape(1, B_QK), jnp.tile(gnorm, B_HEADS).reshape(1, B_OUT))
    return o.reshape(b * s, B_OUT)


def token_mixers(pgs, pb, pc, bg, bias, w_gla_gate, b_gla_gate, gla_norm, attn_sinks, b, s):
    outs, lses = [], []
    for gi, (_, dilation) in enumerate(DIL_GROUPS):
        o, lse = attn_dilated_group(pgs[gi], bias, gi, dilation, b, s)
        outs.append(o)
        lses.append(lse)
    o_a = lse_mix(outs, lses)
    o_b = gla(pb, bg, w_gla_gate, b_gla_gate, gla_norm, b, s)
    o_c = attn_swa(pc, bias, attn_sinks, b, s)
    return o_a, o_b, o_c


def _cast_kernel(w_ref, o_ref):
    o_ref[...] = w_ref[...].astype(o_ref.dtype)


def cast_layer(w, l, *, tr=256):
    _, rows, cols = w.shape
    return pl.pallas_call(
        _cast_kernel, grid=(rows // tr,),
        in_specs=[pl.BlockSpec((None, tr, cols), lambda i: (l, i, 0))],
        out_specs=pl.BlockSpec((tr, cols), lambda i: (i, 0)),
        out_shape=jax.ShapeDtypeStruct((rows, cols), BF16),
        compiler_params=_cparams(("parallel",)), name="cast_w",
    )(w)


def _cast_halves_kernel(w_ref, lo_ref, hi_ref):
    half = lo_ref.shape[1]
    lo_ref[...] = w_ref[:, :half].astype(lo_ref.dtype)
    hi_ref[...] = w_ref[:, half:].astype(hi_ref.dtype)


def cast_halves(w, l, *, tr=64):
    _, rows, cols = w.shape
    half = cols // 2
    out = pl.BlockSpec((tr, half), lambda i: (i, 0))
    sds = jax.ShapeDtypeStruct((rows, half), BF16)
    return pl.pallas_call(
        _cast_halves_kernel, grid=(rows // tr,),
        in_specs=[pl.BlockSpec((None, tr, cols), lambda i: (l, i, 0))],
        out_specs=[out, out], out_shape=[sds, sds],
        compiler_params=_cparams(("parallel",)), name="cast_halves",
    )(w)


A_SEC = A_HEADS * HEAD_DIM


W_IN_STRIP = 128


def _cast_w_in_kernel(w_ref, a0_ref, a1_ref, a2_ref, b_ref, bg_ref, c_ref, g_ref):
    cvt = lambda lo, n: w_ref[lo:lo + n, :].astype(BF16)
    for gi, a_ref in enumerate((a0_ref, a1_ref, a2_ref)):
        for sec in range(3):
            a_ref[sec * STEP_COLS:(sec + 1) * STEP_COLS, :] = cvt(sec * A_SEC + gi * STEP_COLS, STEP_COLS)
    b_ref[...] = cvt(OFF_B, B_COLS)
    bg_ref[...] = cvt(OFF_BG, B_GATE_RANK)
    c_ref[...] = cvt(OFF_C, C_COLS)
    g_ref[...] = cvt(OFF_G, G_COLS)


def cast_w_in(w_in_t, l):
    _, n_in, d = w_in_t.shape
    heights = (3 * STEP_COLS,) * len(DIL_GROUPS) + (B_COLS, B_GATE_RANK, C_COLS, G_COLS)
    return pl.pallas_call(
        _cast_w_in_kernel, grid=(d // W_IN_STRIP,),
        in_specs=[pl.BlockSpec((None, n_in, W_IN_STRIP), lambda j: (l, 0, j))],
        out_specs=[pl.BlockSpec((n, W_IN_STRIP), lambda j: (0, j)) for n in heights],
        out_shape=[jax.ShapeDtypeStruct((n, d), BF16) for n in heights],
        compiler_params=_cparams(("parallel",)), name="cast_w_in",
    )(w_in_t)


def kernel(x, rel_bias, w_in, w_gla_gate, b_gla_gate, gla_norm, attn_sinks, w_br_a, w_br_b, w_br_c, w_out,
           g_pre_mix, g_post_mix, g_pre_ffn, g_post_ffn, w_up, conv_w, conv_b, w_down):
    b, s, d = x.shape
    depth = w_in.shape[0]
    xf = x.reshape(b * s, d)
    xn = rms_cast(xf, g_pre_mix[0])
    bias = bias_tiles(rel_bias)
    w_in_t = jnp.swapaxes(w_in, 1, 2)
    for l in range(depth):
        *w_a, w_b, w_bg, w_c, w_g = cast_w_in(w_in_t, l)
        proj = functools.partial(matmul, xn, trans_b=True)
        pgs = [proj(w_a[gi], tn=768, out_dtype=BF16 if dil == 1 else F32, name=f"proj_a{gi}")
               for gi, (_, dil) in enumerate(DIL_GROUPS)]
        pb = proj(w_b, name="proj_b")
        pc = proj(w_c, tn=768, name="proj_c")
        gates = proj(w_g, act="sigmoid", name="proj_g")
        bg = proj(w_bg, out_dtype=F32, name="proj_bg")

        o_a, o_b, o_c = token_mixers(pgs, pb, pc, bg, bias, w_gla_gate[l], b_gla_gate[l], gla_norm[l],
                                     attn_sinks[l], b, s)

        merged = merge(o_a, o_b, o_c, gates, cast_layer(w_br_a, l), cast_layer(w_br_b, l), cast_layer(w_br_c, l))
        y = matmul(merged, cast_layer(w_out, l), out_dtype=F32, name="proj_out")
        xf, hn = norm_residual(y, xf, g_post_mix[l], g_pre_ffn[l])

        w_gate, w_upper = cast_halves(w_up, l)
        h = ffn_up(hn, w_gate, w_upper, conv_w[l][:, :D_FF], conv_w[l][:, D_FF:],
                   conv_b[l][:D_FF].reshape(1, D_FF), conv_b[l][D_FF:].reshape(1, D_FF), seq=s)
        f = matmul_k(h, cast_layer(w_down, l), tn=512, tk=D_FF // 2, name="ffn_down")
        xf, xn = norm_residual(f, xf, g_post_ffn[l], g_pre_mix[l + 1] if l + 1 < depth else None)
    return xf.reshape(b, s, d)
```

```python
import functools
import math

import jax
import jax.numpy as jnp
from jax import lax
from jax.experimental import pallas as pl
from jax.experimental.pallas import tpu as pltpu

F32 = jnp.float32
BF16 = jnp.bfloat16

D_MODEL = 4096
HEAD_DIM = 128
BLOCK = 128
NORM_EPS = 1e-6
DIL_GROUPS = ((128, 1), (512, 4), (2048, 16))
A_HEADS_PER_GROUP = 4
A_HEADS = 12
A_OUT = 512
B_HEADS = 8
B_DK = 64
B_DV = 128
B_GATE_RANK = 16
B_GATE_TAU = 16.0
B_CHUNK = 64
B_OUT = 1024
C_Q_HEADS = 12
C_KV_HEADS = 3
C_WINDOW = 128
C_OUT = 1536
REL_BUCKETS = 32
REL_MAX_DIST = 2048
D_FF = 11008
CONV_W = 3

A_COLS = 3 * A_HEADS * HEAD_DIM
B_COLS = 2 * B_HEADS * B_DK + 2 * B_HEADS * B_DV
C_COLS = (C_Q_HEADS + 2 * C_KV_HEADS) * HEAD_DIM
G_COLS = 3 * D_MODEL
OFF_B = A_COLS
OFF_BG = OFF_B + B_COLS
OFF_C = OFF_BG + B_GATE_RANK
OFF_G = OFF_C + C_COLS


VMEM_LIMIT = 56 * 1024 * 1024


def _cparams(sem):
    return pltpu.CompilerParams(dimension_semantics=sem, vmem_limit_bytes=VMEM_LIMIT)


ROW_PARTS = 4


def _row_parts(tm):
    n = ROW_PARTS if tm % (ROW_PARTS * 128) == 0 else 1
    return [pl.ds(p * (tm // n), tm // n) for p in range(n)]


def _rms(x, g):
    ms = jnp.mean(x * x, axis=-1, keepdims=True)
    return x * lax.rsqrt(ms + NORM_EPS) * g


def _rms_cast_kernel(x_ref, g_ref, o_ref):
    o_ref[...] = _rms(x_ref[...], g_ref[...]).astype(o_ref.dtype)


def rms_cast(x, g, *, tr=256):
    m, d = x.shape
    return pl.pallas_call(
        _rms_cast_kernel,
        grid=(m // tr,),
        in_specs=[pl.BlockSpec((tr, d), lambda i: (i, 0)),
                  pl.BlockSpec((1, d), lambda i: (0, 0))],
        out_specs=pl.BlockSpec((tr, d), lambda i: (i, 0)),
        out_shape=jax.ShapeDtypeStruct((m, d), BF16),
        compiler_params=_cparams(("parallel",)),
        name="rms_cast",
    )(x, g.reshape(1, d))


def _norm_res_kernel(y_ref, x_ref, g_ref, g2_ref, xo_ref, hn_ref):
    xn = x_ref[...] + _rms(y_ref[...].astype(F32), g_ref[...])
    xo_ref[...] = xn
    hn_ref[...] = _rms(xn, g2_ref[...]).astype(hn_ref.dtype)


def _norm_res_last_kernel(y_ref, x_ref, g_ref, xo_ref):
    xo_ref[...] = x_ref[...] + _rms(y_ref[...].astype(F32), g_ref[...])


def norm_residual(y, x, g, g_next=None, *, tr=256):
    m, d = x.shape
    row = pl.BlockSpec((tr, d), lambda i: (i, 0))
    vec = pl.BlockSpec((1, d), lambda i: (0, 0))
    if g_next is None:
        return pl.pallas_call(
            _norm_res_last_kernel, grid=(m // tr,),
            in_specs=[row, row, vec], out_specs=row,
            out_shape=jax.ShapeDtypeStruct((m, d), F32),
            compiler_params=_cparams(("parallel",)), name="norm_res_last",
        )(y, x, g.reshape(1, d)), None
    return pl.pallas_call(
        _norm_res_kernel, grid=(m // tr,),
        in_specs=[row, row, vec, vec], out_specs=[row, row],
        out_shape=[jax.ShapeDtypeStruct((m, d), F32), jax.ShapeDtypeStruct((m, d), BF16)],
        compiler_params=_cparams(("parallel",)), name="norm_res",
    )(y, x, g.reshape(1, d), g_next.reshape(1, d))


def _mm_kernel(a_ref, w_ref, o_ref, *, act, trans_b):
    contract = (((1,), (1 if trans_b else 0,)), ((), ()))
    w = w_ref[...]
    for rows in _row_parts(a_ref.shape[0]):
        acc = lax.dot_general(a_ref[rows, :], w, contract, preferred_element_type=F32)
        if act == "sigmoid":
            acc = jax.nn.sigmoid(acc)
        o_ref[rows, :] = acc.astype(o_ref.dtype)


def matmul(a, w, *, trans_b=False, tm=1024, tn=1024, out_dtype=BF16, act=None, name="mm"):
    m, k = a.shape
    n = w.shape[0 if trans_b else 1]
    tm = min(tm, m)
    tn = min(tn, n)
    assert m % tm == 0 and n % tn == 0
    w_spec = pl.BlockSpec((tn, k), lambda i, j: (j, 0)) if trans_b else pl.BlockSpec((k, tn), lambda i, j: (0, j))
    return pl.pallas_call(
        functools.partial(_mm_kernel, act=act, trans_b=trans_b),
        grid=(m // tm, n // tn),
        in_specs=[pl.BlockSpec((tm, k), lambda i, j: (i, 0)), w_spec],
        out_specs=pl.BlockSpec((tm, tn), lambda i, j: (i, j)),
        out_shape=jax.ShapeDtypeStruct((m, n), out_dtype),
        compiler_params=_cparams(("parallel", "arbitrary")),
        name=name,
    )(a, w)


def _merge_kernel(oa_ref, ob_ref, oc_ref, ga_ref, gb_ref, gc_ref, wa_ref, wb_ref, wc_ref, o_ref):
    wa, wb, wc = wa_ref[...], wb_ref[...], wc_ref[...]
    for rows in _row_parts(o_ref.shape[0]):
        gated = lambda g_ref, o_in_ref, w: g_ref[rows, :].astype(F32) * jnp.dot(o_in_ref[rows, :], w,
                                                                                preferred_element_type=F32)
        acc = gated(ga_ref, oa_ref, wa)
        acc = acc + gated(gb_ref, ob_ref, wb)
        acc = acc + gated(gc_ref, oc_ref, wc)
        o_ref[rows, :] = acc.astype(o_ref.dtype)


def merge(o_a, o_b, o_c, gates, wa, wb, wc, *, tm=1024, tn=1024):
    m = o_a.shape[0]
    d = wa.shape[1]
    tm = min(tm, m)
    nj = d // tn
    act = lambda kdim: pl.BlockSpec((tm, kdim), lambda i, j: (i, 0))
    gate = lambda s: pl.BlockSpec((tm, tn), lambda i, j: (i, s * nj + j))
    wgt = lambda kdim: pl.BlockSpec((kdim, tn), lambda i, j: (0, j))
    return pl.pallas_call(
        _merge_kernel,
        grid=(m // tm, nj),
        in_specs=[act(o_a.shape[1]), act(o_b.shape[1]), act(o_c.shape[1]),
                  gate(0), gate(1), gate(2),
                  wgt(wa.shape[0]), wgt(wb.shape[0]), wgt(wc.shape[0])],
        out_specs=pl.BlockSpec((tm, tn), lambda i, j: (i, j)),
        out_shape=jax.ShapeDtypeStruct((m, d), BF16),
        compiler_params=_cparams(("parallel", "arbitrary")),
        name="merge",
    )(o_a, o_b, o_c, gates, gates, gates, wa, wb, wc)


TAIL = 8


def _ffn_up_kernel(hn_ref, wg_ref, wu_ref, cwg_ref, cwu_ref, cbg_ref, cbu_ref, o_ref, tail_g, tail_u, *,
                   tiles_per_seq):
    tm = o_ref.shape[0]
    j = pl.program_id(1)

    @pl.when(pl.program_id(0) % tiles_per_seq == 0)
    def _():
        tail_g[j] = jnp.zeros(tail_g.shape[1:], F32)
        tail_u[j] = jnp.zeros(tail_u.shape[1:], F32)

    a = hn_ref[...]

    def conv(w_ref, cw_ref, cb_ref, tail_ref):
        w = w_ref[...]
        part = tm // ROW_PARTS
        u = jnp.concatenate([jnp.dot(a[p * part:(p + 1) * part], w, preferred_element_type=F32)
                             for p in range(ROW_PARTS)], axis=0)
        ext = jnp.concatenate([tail_ref[j], u], axis=0)
        tail_ref[j] = u[tm - TAIL:]
        cw = cw_ref[...]
        y = cb_ref[...] + cw[0:1] * ext[TAIL - 2:TAIL - 2 + tm]
        y = y + cw[1:2] * ext[TAIL - 1:TAIL - 1 + tm]
        return y + cw[2:3] * ext[TAIL:TAIL + tm]

    gate = conv(wg_ref, cwg_ref, cbg_ref, tail_g)
    up = conv(wu_ref, cwu_ref, cbu_ref, tail_u)
    o_ref[...] = (jax.nn.silu(gate) * up).astype(o_ref.dtype)


def ffn_up(hn, w_gate, w_upper, cw_gate, cw_upper, cb_gate, cb_upper, *, seq, tm=1024, tn=512):
    m, d = hn.shape
    ff = w_gate.shape[1]
    tm = min(tm, seq)
    assert seq % tm == 0
    nj = pl.cdiv(ff, tn)
    col = lambda rows: pl.BlockSpec((rows, tn), lambda i, j: (0, j))
    return pl.pallas_call(
        functools.partial(_ffn_up_kernel, tiles_per_seq=seq // tm),
        grid=(m // tm, nj),
        in_specs=[pl.BlockSpec((tm, d), lambda i, j: (i, 0)),
                  col(d), col(d), col(CONV_W), col(CONV_W), col(1), col(1)],
        out_specs=pl.BlockSpec((tm, tn), lambda i, j: (i, j)),
        out_shape=jax.ShapeDtypeStruct((m, ff), BF16),
        scratch_shapes=[pltpu.VMEM((nj, TAIL, tn), F32), pltpu.VMEM((nj, TAIL, tn), F32)],
        compiler_params=_cparams(("arbitrary", "arbitrary")),
        name="ffn_up",
    )(hn, w_gate, w_upper, cw_gate, cw_upper, cb_gate, cb_upper)


NEG = -1e30
N_BIAS_HEADS = A_HEADS + C_Q_HEADS


def _bucket_tiles():
    qi = jnp.arange(BLOCK)[:, None]
    kj = jnp.arange(2 * BLOCK)[None, :]
    rel = qi + BLOCK - kj
    max_exact = REL_BUCKETS // 2
    tiles = []
    for dilation, span in [(d, w // d) for w, d in DIL_GROUPS] + [(1, C_WINDOW - 1)]:
        dist = jnp.maximum(rel * dilation, 0)
        far = max_exact + (jnp.log(jnp.maximum(dist, 1).astype(F32) / max_exact)
                           / math.log(REL_MAX_DIST / max_exact) * (REL_BUCKETS - max_exact)).astype(jnp.int32)
        bucket = jnp.where(dist < max_exact, dist, jnp.minimum(far, REL_BUCKETS - 1))
        tiles.append(jnp.where((rel >= 0) & (rel <= span), bucket, -1))
    return jnp.stack(tiles).astype(jnp.int32)


def _bias_kernel(tab_ref, idx_ref, o_ref):
    h = pl.program_id(0)
    idx = idx_ref[...]
    acc = jnp.full(idx.shape, NEG, F32)
    for bkt in range(REL_BUCKETS):
        acc = jnp.where(idx == bkt, tab_ref[bkt, h], acc)
    col = lax.broadcasted_iota(jnp.int32, idx.shape, 1)
    o_ref[0] = jnp.where(col < BLOCK, NEG, acc)
    o_ref[1] = acc


def bias_tiles(rel_bias):
    groups = len(DIL_GROUPS)
    return pl.pallas_call(
        _bias_kernel,
        grid=(N_BIAS_HEADS,),
        in_specs=[pl.BlockSpec(memory_space=pltpu.SMEM),
                  pl.BlockSpec((None, BLOCK, 2 * BLOCK),
                               lambda h: (jnp.minimum(h // A_HEADS_PER_GROUP, groups), 0, 0))],
        out_specs=pl.BlockSpec((2, None, BLOCK, 2 * BLOCK), lambda h: (0, h, 0, 0)),
        out_shape=jax.ShapeDtypeStruct((2, N_BIAS_HEADS, BLOCK, 2 * BLOCK), F32),
        compiler_params=_cparams(("arbitrary",)),
        name="bias_tiles",
    )(rel_bias, _bucket_tiles())


ATT_SCALE = HEAD_DIM ** -0.5
HEADS_PER_STEP = 4
STEP_COLS = HEADS_PER_STEP * HEAD_DIM


def _scores(q, kp, kc, bias):
    k = jnp.concatenate([kp, kc], axis=0)
    s = lax.dot_general(q, k, (((1,), (1,)), ((), ())), preferred_element_type=F32)
    return s * ATT_SCALE + bias


def _attn_a_kernel(q_ref, kc_ref, kp_ref, vc_ref, vp_ref, bias_ref, o_ref, lse_ref, *, dilation, heads, qblocks):
    span = BLOCK * dilation
    first = jnp.minimum(pl.program_id(1), 1)
    for sb in range(qblocks):
        for r in range(dilation):
            def rows(blk):
                start = blk * span + r
                return pl.ds(start, BLOCK, stride=dilation) if dilation > 1 else pl.ds(start, BLOCK)

            for h in range(heads):
                sl = slice(h * HEAD_DIM, (h + 1) * HEAD_DIM)
                ld = lambda ref, blk: ref[rows(blk), sl].astype(BF16)
                prev = (lambda ref_p, ref_c: ld(ref_p, 0)) if sb == 0 else (lambda ref_p, ref_c: ld(ref_c, sb - 1))
                bias = bias_ref[first, h] if sb == 0 else bias_ref[1, h]
                s = _scores(ld(q_ref, sb), prev(kp_ref, kc_ref), ld(kc_ref, sb), bias)
                m = jnp.max(s, axis=-1, keepdims=True)
                p = jnp.exp(s - m)
                l = jnp.sum(p, axis=-1, keepdims=True)
                v = jnp.concatenate([prev(vp_ref, vc_ref), ld(vc_ref, sb)], axis=0)
                o = jnp.dot(p.astype(BF16), v, preferred_element_type=F32)
                o_ref[rows(sb), sl] = o / l
                lse_ref[rows(sb), sl] = jnp.broadcast_to(m + jnp.log(l), (BLOCK, HEAD_DIM))


def attn_dilated_group(pg, bias, gi, dilation, b, s):
    span = BLOCK * dilation
    qblocks = 4 if dilation < 16 else 1
    rows = span * qblocks
    heads = HEADS_PER_STEP if dilation == 1 else 1
    cols = heads * HEAD_DIM
    hs = HEADS_PER_STEP // heads
    pgv = pg.reshape(b, s, 3 * STEP_COLS)

    def cur(sec):
        return pl.BlockSpec((None, rows, cols), lambda bi, n, h: (bi, n, sec * hs + h))

    def prev(sec):
        return pl.BlockSpec((None, span, cols), lambda bi, n, h: (bi, jnp.maximum(n * qblocks - 1, 0), sec * hs + h))

    out_spec = pl.BlockSpec((None, rows, cols), lambda bi, n, h: (bi, n, h))
    out_sds = jax.ShapeDtypeStruct((b, s, STEP_COLS), F32)
    o, lse = pl.pallas_call(
        functools.partial(_attn_a_kernel, dilation=dilation, heads=heads, qblocks=qblocks),
        grid=(b, s // rows, hs),
        in_specs=[cur(0), cur(1), prev(1), cur(2), prev(2),
                  pl.BlockSpec((2, heads, BLOCK, 2 * BLOCK), lambda bi, n, h: (0, gi * hs + h, 0, 0))],
        out_specs=[out_spec, out_spec],
        out_shape=[out_sds, out_sds],
        compiler_params=_cparams(("parallel", "arbitrary", "arbitrary")),
        name=f"attn_dil{dilation}",
    )(pgv, pgv, pgv, pgv, pgv, bias)
    return o.reshape(b * s, STEP_COLS), lse.reshape(b * s, STEP_COLS)


def _lse_mix_kernel(o0_ref, o1_ref, o2_ref, l0_ref, l1_ref, l2_ref, o_ref):
    l0, l1, l2 = l0_ref[...], l1_ref[...], l2_ref[...]
    mx = jnp.maximum(jnp.maximum(l0, l1), l2)
    e0, e1, e2 = jnp.exp(l0 - mx), jnp.exp(l1 - mx), jnp.exp(l2 - mx)
    num = e0 * o0_ref[...] + e1 * o1_ref[...] + e2 * o2_ref[...]
    o_ref[...] = (num / (e0 + e1 + e2)).astype(o_ref.dtype)


def lse_mix(outs, lses, *, tr=1024):
    m, c = outs[0].shape
    tr = min(tr, m)
    spec = pl.BlockSpec((tr, c), lambda i: (i, 0))
    return pl.pallas_call(
        _lse_mix_kernel, grid=(m // tr,),
        in_specs=[spec] * 6, out_specs=spec,
        out_shape=jax.ShapeDtypeStruct((m, c), BF16),
        compiler_params=_cparams(("parallel",)), name="lse_mix",
    )(*outs, *lses)


SWA_QBLOCKS = 4


def _attn_c_kernel(sink_ref, q_ref, kc_ref, kp_ref, vc_ref, vp_ref, bias_ref, o_ref):
    kh = pl.program_id(2)
    first = jnp.minimum(pl.program_id(1), 1)
    for sb in range(SWA_QBLOCKS):
        rows = pl.ds(sb * BLOCK, BLOCK)
        before = pl.ds((sb - 1) * BLOCK, BLOCK)
        kprev, vprev = (kp_ref[...], vp_ref[...]) if sb == 0 else (kc_ref[before, :], vc_ref[before, :])
        v = jnp.concatenate([vprev, vc_ref[rows, :]], axis=0)
        for gq in range(HEADS_PER_STEP):
            sl = slice(gq * HEAD_DIM, (gq + 1) * HEAD_DIM)
            bias = bias_ref[first, gq] if sb == 0 else bias_ref[1, gq]
            s = _scores(q_ref[rows, sl], kprev, kc_ref[rows, :], bias)
            sink = sink_ref[kh * HEADS_PER_STEP + gq]
            m = jnp.maximum(jnp.max(s, axis=-1, keepdims=True), sink)
            w = jnp.exp(s - m)
            den = jnp.sum(w, axis=-1, keepdims=True) + jnp.exp(sink - m)
            o = jnp.dot(w.astype(BF16), v, preferred_element_type=F32)
            o_ref[rows, sl] = (o / den).astype(o_ref.dtype)


def attn_swa(pc, bias, sinks, b, s):
    rows = BLOCK * SWA_QBLOCKS
    pcv = pc.reshape(b, s, C_COLS)
    koff = C_Q_HEADS
    voff = C_Q_HEADS + C_KV_HEADS

    def cur(off):
        return pl.BlockSpec((None, rows, HEAD_DIM), lambda bi, n, kh: (bi, n, off + kh))

    def prev(off):
        return pl.BlockSpec((None, BLOCK, HEAD_DIM),
                            lambda bi, n, kh: (bi, jnp.maximum(n * SWA_QBLOCKS - 1, 0), off + kh))

    o = pl.pallas_call(
        _attn_c_kernel,
        grid=(b, s // rows, C_KV_HEADS),
        in_specs=[pl.BlockSpec(memory_space=pltpu.SMEM),
                  pl.BlockSpec((None, rows, STEP_COLS), lambda bi, n, kh: (bi, n, kh)),
                  cur(koff), prev(koff), cur(voff), prev(voff),
                  pl.BlockSpec((2, HEADS_PER_STEP, BLOCK, 2 * BLOCK),
                               lambda bi, n, kh: (0, A_HEADS // HEADS_PER_STEP + kh, 0, 0))],
        out_specs=pl.BlockSpec((None, rows, STEP_COLS), lambda bi, n, kh: (bi, n, kh)),
        out_shape=jax.ShapeDtypeStruct((b, s, C_OUT), BF16),
        compiler_params=_cparams(("parallel", "arbitrary", "arbitrary")),
        name="attn_swa",
    )(sinks, pcv, pcv, pcv, pcv, pcv, bias)
    return o.reshape(b * s, C_OUT)


B_QK = B_HEADS * B_DK


def _split3(x):
    hi = x.astype(BF16)
    r1 = x - hi.astype(F32)
    mid = r1.astype(BF16)
    lo = (r1 - mid.astype(F32)).astype(BF16)
    return hi, mid, lo


GLA_SUB = 4


def _gla_kernel(q_ref, k_ref, v_ref, r_ref, bg_ref, wg_ref, bgate_ref, gnorm_ref, o_ref, state_ref):
    @pl.when(pl.program_id(1) == 0)
    def _():
        state_ref[...] = jnp.zeros_like(state_ref)

    row = lax.broadcasted_iota(jnp.int32, (B_CHUNK, B_CHUNK), 0)
    coli = lax.broadcasted_iota(jnp.int32, (B_CHUNK, B_CHUNK), 1)
    causal = row >= coli
    tril = jnp.where(causal, 1.0, 0.0).astype(BF16)

    for sc in range(GLA_SUB):
        rows = pl.ds(sc * B_CHUNK, B_CHUNK)
        z = jnp.dot(bg_ref[rows, :].astype(BF16), wg_ref[...], preferred_element_type=F32) + bgate_ref[...]
        log_a = (jnp.minimum(z, 0.0) - jnp.log(1.0 + jnp.exp(-jnp.abs(z)))) / B_GATE_TAU
        cum = sum(jnp.dot(tril, part, preferred_element_type=F32) for part in _split3(log_a))
        last = cum[B_CHUNK - 1:B_CHUNK, :]

        qf = q_ref[rows, :].astype(F32) * (B_DK ** -0.5)
        kf = k_ref[rows, :].astype(F32)
        q_dec = (qf * jnp.exp(cum)).astype(BF16)
        k_inv = (kf * jnp.exp(-cum)).astype(BF16)
        k_out = (kf * jnp.exp(last - cum)).astype(BF16)
        decay = jnp.exp(last)

        for h in range(B_HEADS):
            ks = slice(h * B_DK, (h + 1) * B_DK)
            vs = slice(h * B_DV, (h + 1) * B_DV)
            qd, ki, ko, vh = q_dec[:, ks], k_inv[:, ks], k_out[:, ks], v_ref[rows, vs]
            att = lax.dot_general(qd, ki, (((1,), (1,)), ((), ())), preferred_element_type=F32)
            att = jnp.where(causal, att, 0.0).astype(BF16)
            st = state_ref[h]
            o = jnp.dot(att, vh, preferred_element_type=F32)
            o = o + lax.dot_general(qd, st.astype(BF16), (((1,), (1,)), ((), ())), preferred_element_type=F32)
            upd = lax.dot_general(vh, ko, (((0,), (0,)), ((), ())), preferred_element_type=F32)
            state_ref[h] = st * decay[:, ks] + upd
            on = _rms(o, gnorm_ref[:, vs])
            o_ref[rows, vs] = (on * jax.nn.silu(r_ref[rows, vs].astype(F32))).astype(o_ref.dtype)


def gla(pb, bg, w_gate, b_gate, gnorm, b, s):
    rows = B_CHUNK * GLA_SUB
    pbv = pb.reshape(b, s, B_COLS)
    bgv = bg.reshape(b, s, B_GATE_RANK)
    wg = w_gate.astype(BF16)
    full = lambda shape: pl.BlockSpec(shape, lambda bi, c: (0, 0))
    o = pl.pallas_call(
        _gla_kernel,
        grid=(b, s // rows),
        in_specs=[pl.BlockSpec((None, rows, B_QK), lambda bi, c: (bi, c, 0)),
                  pl.BlockSpec((None, rows, B_QK), lambda bi, c: (bi, c, 1)),
                  pl.BlockSpec((None, rows, B_OUT), lambda bi, c: (bi, c, 1)),
                  pl.BlockSpec((None, rows, B_OUT), lambda bi, c: (bi, c, 2)),
                  pl.BlockSpec((None, rows, B_GATE_RANK), lambda bi, c: (bi, c, 0)),
                  full((B_GATE_RANK, B_QK)), full((1, B_QK)), full((1, B_OUT))],
        out_specs=pl.BlockSpec((None, rows, B_OUT), lambda bi, c: (bi, c, 0)),
        out_shape=jax.ShapeDtypeStruct((b, s, B_OUT), BF16),
        scratch_shapes=[pltpu.VMEM((B_HEADS, B_DV, B_DK), F32)],
        compiler_params=_cparams(("parallel", "arbitrary")),
        name="gla",
    )(pbv, pbv, pbv, pbv, bgv, wg, b_gate.reshape(1, B_QK), jnp.tile(gnorm, B_HEADS).reshape(1, B_OUT))
    return o.reshape(b * s, B_OUT)


def token_mixers(pgs, pb, pc, bg, bias, w_gla_gate, b_gla_gate, gla_norm, attn_sinks, b, s):
    outs, lses = [], []
    for gi, (_, dilation) in enumerate(DIL_GROUPS):
        o, lse = attn_dilated_group(pgs[gi], bias, gi, dilation, b, s)
        outs.append(o)
        lses.append(lse)
    o_a = lse_mix(outs, lses)
    o_b = gla(pb, bg, w_gla_gate, b_gla_gate, gla_norm, b, s)
    o_c = attn_swa(pc, bias, attn_sinks, b, s)
    return o_a, o_b, o_c


def _cast_kernel(w_ref, o_ref):
    o_ref[...] = w_ref[...].astype(o_ref.dtype)


def cast_layer(w, l, *, tr=256):
    _, rows, cols = w.shape
    return pl.pallas_call(
        _cast_kernel, grid=(rows // tr,),
        in_specs=[pl.BlockSpec((None, tr, cols), lambda i: (l, i, 0))],
        out_specs=pl.BlockSpec((tr, cols), lambda i: (i, 0)),
        out_shape=jax.ShapeDtypeStruct((rows, cols), BF16),
        compiler_params=_cparams(("parallel",)), name="cast_w",
    )(w)


def _cast_halves_kernel(w_ref, lo_ref, hi_ref):
    half = lo_ref.shape[1]
    lo_ref[...] = w_ref[:, :half].astype(lo_ref.dtype)
    hi_ref[...] = w_ref[:, half:].astype(hi_ref.dtype)


def cast_halves(w, l, *, tr=64):
    _, rows, cols = w.shape
    half = cols // 2
    out = pl.BlockSpec((tr, half), lambda i: (i, 0))
    sds = jax.ShapeDtypeStruct((rows, half), BF16)
    return pl.pallas_call(
        _cast_halves_kernel, grid=(rows // tr,),
        in_specs=[pl.BlockSpec((None, tr, cols), lambda i: (l, i, 0))],
        out_specs=[out, out], out_shape=[sds, sds],
        compiler_params=_cparams(("parallel",)), name="cast_halves",
    )(w)


A_SEC = A_HEADS * HEAD_DIM


W_IN_STRIP = 128


def _cast_w_in_kernel(w_ref, a0_ref, a1_ref, a2_ref, b_ref, bg_ref, c_ref, g_ref):
    cvt = lambda lo, n: w_ref[lo:lo + n, :].astype(BF16)
    for gi, a_ref in enumerate((a0_ref, a1_ref, a2_ref)):
        for sec in range(3):
            a_ref[sec * STEP_COLS:(sec + 1) * STEP_COLS, :] = cvt(sec * A_SEC + gi * STEP_COLS, STEP_COLS)
    b_ref[...] = cvt(OFF_B, B_COLS)
    bg_ref[...] = cvt(OFF_BG, B_GATE_RANK)
    c_ref[...] = cvt(OFF_C, C_COLS)
    g_ref[...] = cvt(OFF_G, G_COLS)


def cast_w_in(w_in_t, l):
    _, n_in, d = w_in_t.shape
    heights = (3 * STEP_COLS,) * len(DIL_GROUPS) + (B_COLS, B_GATE_RANK, C_COLS, G_COLS)
    return pl.pallas_call(
        _cast_w_in_kernel, grid=(d // W_IN_STRIP,),
        in_specs=[pl.BlockSpec((None, n_in, W_IN_STRIP), lambda j: (l, 0, j))],
        out_specs=[pl.BlockSpec((n, W_IN_STRIP), lambda j: (0, j)) for n in heights],
        out_shape=[jax.ShapeDtypeStruct((n, d), BF16) for n in heights],
        compiler_params=_cparams(("parallel",)), name="cast_w_in",
    )(w_in_t)


def kernel(x, rel_bias, w_in, w_gla_gate, b_gla_gate, gla_norm, attn_sinks, w_br_a, w_br_b, w_br_c, w_out,
           g_pre_mix, g_post_mix, g_pre_ffn, g_post_ffn, w_up, conv_w, conv_b, w_down):
    b, s, d = x.shape
    depth = w_in.shape[0]
    xf = x.reshape(b * s, d)
    xn = rms_cast(xf, g_pre_mix[0])
    bias = bias_tiles(rel_bias)
    w_in_t = jnp.swapaxes(w_in, 1, 2)
    for l in range(depth):
        *w_a, w_b, w_bg, w_c, w_g = cast_w_in(w_in_t, l)
        proj = functools.partial(matmul, xn, trans_b=True)
        pgs = [proj(w_a[gi], tn=768, out_dtype=BF16 if dil == 1 else F32, name=f"proj_a{gi}")
               for gi, (_, dil) in enumerate(DIL_GROUPS)]
        pb = proj(w_b, name="proj_b")
        pc = proj(w_c, tn=768, name="proj_c")
        gates = proj(w_g, act="sigmoid", name="proj_g")
        bg = proj(w_bg, out_dtype=F32, name="proj_bg")

        o_a, o_b, o_c = token_mixers(pgs, pb, pc, bg, bias, w_gla_gate[l], b_gla_gate[l], gla_norm[l],
                                     attn_sinks[l], b, s)

        merged = merge(o_a, o_b, o_c, gates, cast_layer(w_br_a, l), cast_layer(w_br_b, l), cast_layer(w_br_c, l))
        y = matmul(merged, cast_layer(w_out, l), out_dtype=F32, name="proj_out")
        xf, hn = norm_residual(y, xf, g_post_mix[l], g_pre_ffn[l])

        w_gate, w_upper = cast_halves(w_up, l)
        h = ffn_up(hn, w_gate, w_upper, conv_w[l][:, :D_FF], conv_w[l][:, D_FF:],
                   conv_b[l][:D_FF].reshape(1, D_FF), conv_b[l][D_FF:].reshape(1, D_FF), seq=s)
        f = matmul(h, cast_layer(w_down, l), tm=512, tn=512, out_dtype=F32, name="ffn_down")
        xf, xn = norm_residual(f, xf, g_post_ffn[l], g_pre_mix[l + 1] if l + 1 < depth else None)
    return xf.reshape(b, s, d)
```

```python
import functools
import math

import jax
import jax.numpy as jnp
from jax import lax
from jax.experimental import pallas as pl
from jax.experimental.pallas import tpu as pltpu

F32 = jnp.float32
BF16 = jnp.bfloat16

D_MODEL = 4096
HEAD_DIM = 128
BLOCK = 128
NORM_EPS = 1e-6
DIL_GROUPS = ((128, 1), (512, 4), (2048, 16))
A_HEADS_PER_GROUP = 4
A_HEADS = 12
A_OUT = 512
B_HEADS = 8
B_DK = 64
B_DV = 128
B_GATE_RANK = 16
B_GATE_TAU = 16.0
B_CHUNK = 64
B_OUT = 1024
C_Q_HEADS = 12
C_KV_HEADS = 3
C_WINDOW = 128
C_OUT = 1536
REL_BUCKETS = 32
REL_MAX_DIST = 2048
D_FF = 11008
CONV_W = 3

A_COLS = 3 * A_HEADS * HEAD_DIM
B_COLS = 2 * B_HEADS * B_DK + 2 * B_HEADS * B_DV
C_COLS = (C_Q_HEADS + 2 * C_KV_HEADS) * HEAD_DIM
G_COLS = 3 * D_MODEL
OFF_B = A_COLS
OFF_BG = OFF_B + B_COLS
OFF_C = OFF_BG + B_GATE_RANK
OFF_G = OFF_C + C_COLS


VMEM_LIMIT = 56 * 1024 * 1024


def _cparams(sem):
    return pltpu.CompilerParams(dimension_semantics=sem, vmem_limit_bytes=VMEM_LIMIT)


ROW_PARTS = 4


def _row_parts(tm):
    n = ROW_PARTS if tm % (ROW_PARTS * 128) == 0 else 1
    return [pl.ds(p * (tm // n), tm // n) for p in range(n)]


def _rms(x, g):
    ms = jnp.mean(x * x, axis=-1, keepdims=True)
    return x * lax.rsqrt(ms + NORM_EPS) * g


def _rms_cast_kernel(x_ref, g_ref, o_ref):
    o_ref[...] = _rms(x_ref[...], g_ref[...]).astype(o_ref.dtype)


def rms_cast(x, g, *, tr=256):
    m, d = x.shape
    return pl.pallas_call(
        _rms_cast_kernel,
        grid=(m // tr,),
        in_specs=[pl.BlockSpec((tr, d), lambda i: (i, 0)),
                  pl.BlockSpec((1, d), lambda i: (0, 0))],
        out_specs=pl.BlockSpec((tr, d), lambda i: (i, 0)),
        out_shape=jax.ShapeDtypeStruct((m, d), BF16),
        compiler_params=_cparams(("parallel",)),
        name="rms_cast",
    )(x, g.reshape(1, d))


def _norm_res_kernel(y_ref, x_ref, g_ref, g2_ref, xo_ref, hn_ref):
    xn = x_ref[...] + _rms(y_ref[...].astype(F32), g_ref[...])
    xo_ref[...] = xn
    hn_ref[...] = _rms(xn, g2_ref[...]).astype(hn_ref.dtype)


def _norm_res_last_kernel(y_ref, x_ref, g_ref, xo_ref):
    xo_ref[...] = x_ref[...] + _rms(y_ref[...].astype(F32), g_ref[...])


def norm_residual(y, x, g, g_next=None, *, tr=256):
    m, d = x.shape
    row = pl.BlockSpec((tr, d), lambda i: (i, 0))
    vec = pl.BlockSpec((1, d), lambda i: (0, 0))
    if g_next is None:
        return pl.pallas_call(
            _norm_res_last_kernel, grid=(m // tr,),
            in_specs=[row, row, vec], out_specs=row,
            out_shape=jax.ShapeDtypeStruct((m, d), F32),
            compiler_params=_cparams(("parallel",)), name="norm_res_last",
        )(y, x, g.reshape(1, d)), None
    return pl.pallas_call(
        _norm_res_kernel, grid=(m // tr,),
        in_specs=[row, row, vec, vec], out_specs=[row, row],
        out_shape=[jax.ShapeDtypeStruct((m, d), F32), jax.ShapeDtypeStruct((m, d), BF16)],
        compiler_params=_cparams(("parallel",)), name="norm_res",
    )(y, x, g.reshape(1, d), g_next.reshape(1, d))


def _mm_kernel(a_ref, w_ref, o_ref, *, act, trans_b):
    contract = (((1,), (1 if trans_b else 0,)), ((), ()))
    w = w_ref[...]
    for rows in _row_parts(a_ref.shape[0]):
        acc = lax.dot_general(a_ref[rows, :], w, contract, preferred_element_type=F32)
        if act == "sigmoid":
            acc = jax.nn.sigmoid(acc)
        o_ref[rows, :] = acc.astype(o_ref.dtype)


def matmul(a, w, *, trans_b=False, tm=1024, tn=1024, out_dtype=BF16, act=None, name="mm"):
    m, k = a.shape
    n = w.shape[0 if trans_b else 1]
    tm = min(tm, m)
    tn = min(tn, n)
    assert m % tm == 0 and n % tn == 0
    w_spec = pl.BlockSpec((tn, k), lambda i, j: (j, 0)) if trans_b else pl.BlockSpec((k, tn), lambda i, j: (0, j))
    return pl.pallas_call(
        functools.partial(_mm_kernel, act=act, trans_b=trans_b),
        grid=(m // tm, n // tn),
        in_specs=[pl.BlockSpec((tm, k), lambda i, j: (i, 0)), w_spec],
        out_specs=pl.BlockSpec((tm, tn), lambda i, j: (i, j)),
        out_shape=jax.ShapeDtypeStruct((m, n), out_dtype),
        compiler_params=_cparams(("parallel", "arbitrary")),
        name=name,
    )(a, w)


def _merge_kernel(oa_ref, ob_ref, oc_ref, ga_ref, gb_ref, gc_ref, wa_ref, wb_ref, wc_ref, o_ref):
    wa, wb, wc = wa_ref[...], wb_ref[...], wc_ref[...]
    for rows in _row_parts(o_ref.shape[0]):
        gated = lambda g_ref, o_in_ref, w: g_ref[rows, :].astype(F32) * jnp.dot(o_in_ref[rows, :], w,
                                                                                preferred_element_type=F32)
        acc = gated(ga_ref, oa_ref, wa)
        acc = acc + gated(gb_ref, ob_ref, wb)
        acc = acc + gated(gc_ref, oc_ref, wc)
        o_ref[rows, :] = acc.astype(o_ref.dtype)


def merge(o_a, o_b, o_c, gates, wa, wb, wc, *, tm=1024, tn=1024):
    m = o_a.shape[0]
    d = wa.shape[1]
    tm = min(tm, m)
    nj = d // tn
    act = lambda kdim: pl.BlockSpec((tm, kdim), lambda i, j: (i, 0))
    gate = lambda s: pl.BlockSpec((tm, tn), lambda i, j: (i, s * nj + j))
    wgt = lambda kdim: pl.BlockSpec((kdim, tn), lambda i, j: (0, j))
    return pl.pallas_call(
        _merge_kernel,
        grid=(m // tm, nj),
        in_specs=[act(o_a.shape[1]), act(o_b.shape[1]), act(o_c.shape[1]),
                  gate(0), gate(1), gate(2),
                  wgt(wa.shape[0]), wgt(wb.shape[0]), wgt(wc.shape[0])],
        out_specs=pl.BlockSpec((tm, tn), lambda i, j: (i, j)),
        out_shape=jax.ShapeDtypeStruct((m, d), BF16),
        compiler_params=_cparams(("parallel", "arbitrary")),
        name="merge",
    )(o_a, o_b, o_c, gates, gates, gates, wa, wb, wc)


TAIL = 8


def _ffn_up_kernel(hn_ref, wg_ref, wu_ref, cwg_ref, cwu_ref, cbg_ref, cbu_ref, o_ref, tail_g, tail_u, *,
                   tiles_per_seq):
    tm = o_ref.shape[0]
    j = pl.program_id(1)

    @pl.when(pl.program_id(0) % tiles_per_seq == 0)
    def _():
        tail_g[j] = jnp.zeros(tail_g.shape[1:], F32)
        tail_u[j] = jnp.zeros(tail_u.shape[1:], F32)

    a = hn_ref[...]

    def conv(w_ref, cw_ref, cb_ref, tail_ref):
        w = w_ref[...]
        part = tm // ROW_PARTS
        u = jnp.concatenate([jnp.dot(a[p * part:(p + 1) * part], w, preferred_element_type=F32)
                             for p in range(ROW_PARTS)], axis=0)
        ext = jnp.concatenate([tail_ref[j], u], axis=0)
        tail_ref[j] = u[tm - TAIL:]
        cw = cw_ref[...]
        y = cb_ref[...] + cw[0:1] * ext[TAIL - 2:TAIL - 2 + tm]
        y = y + cw[1:2] * ext[TAIL - 1:TAIL - 1 + tm]
        return y + cw[2:3] * ext[TAIL:TAIL + tm]

    gate = conv(wg_ref, cwg_ref, cbg_ref, tail_g)
    up = conv(wu_ref, cwu_ref, cbu_ref, tail_u)
    o_ref[...] = (jax.nn.silu(gate) * up).astype(o_ref.dtype)


def ffn_up(hn, w_gate, w_upper, cw_gate, cw_upper, cb_gate, cb_upper, *, seq, tm=1024, tn=512):
    m, d = hn.shape
    ff = w_gate.shape[1]
    tm = min(tm, seq)
    assert seq % tm == 0
    nj = pl.cdiv(ff, tn)
    col = lambda rows: pl.BlockSpec((rows, tn), lambda i, j: (0, j))
    return pl.pallas_call(
        functools.partial(_ffn_up_kernel, tiles_per_seq=seq // tm),
        grid=(m // tm, nj),
        in_specs=[pl.BlockSpec((tm, d), lambda i, j: (i, 0)),
                  col(d), col(d), col(CONV_W), col(CONV_W), col(1), col(1)],
        out_specs=pl.BlockSpec((tm, tn), lambda i, j: (i, j)),
        out_shape=jax.ShapeDtypeStruct((m, ff), BF16),
        scratch_shapes=[pltpu.VMEM((nj, TAIL, tn), F32), pltpu.VMEM((nj, TAIL, tn), F32)],
        compiler_params=_cparams(("arbitrary", "arbitrary")),
        name="ffn_up",
    )(hn, w_gate, w_upper, cw_gate, cw_upper, cb_gate, cb_upper)


NEG = -1e30
N_BIAS_HEADS = A_HEADS + C_Q_HEADS


def _bucket_tiles():
    qi = jnp.arange(BLOCK)[:, None]
    kj = jnp.arange(2 * BLOCK)[None, :]
    rel = qi + BLOCK - kj
    max_exact = REL_BUCKETS // 2
    tiles = []
    for dilation, span in [(d, w // d) for w, d in DIL_GROUPS] + [(1, C_WINDOW - 1)]:
        dist = jnp.maximum(rel * dilation, 0)
        far = max_exact + (jnp.log(jnp.maximum(dist, 1).astype(F32) / max_exact)
                           / math.log(REL_MAX_DIST / max_exact) * (REL_BUCKETS - max_exact)).astype(jnp.int32)
        bucket = jnp.where(dist < max_exact, dist, jnp.minimum(far, REL_BUCKETS - 1))
        tiles.append(jnp.where((rel >= 0) & (rel <= span), bucket, -1))
    return jnp.stack(tiles).astype(jnp.int32)


def _bias_kernel(tab_ref, idx_ref, o_ref):
    h = pl.program_id(0)
    idx = idx_ref[...]
    acc = jnp.full(idx.shape, NEG, F32)
    for bkt in range(REL_BUCKETS):
        acc = jnp.where(idx == bkt, tab_ref[bkt, h], acc)
    col = lax.broadcasted_iota(jnp.int32, idx.shape, 1)
    o_ref[0] = jnp.where(col < BLOCK, NEG, acc)
    o_ref[1] = acc


def bias_tiles(rel_bias):
    groups = len(DIL_GROUPS)
    return pl.pallas_call(
        _bias_kernel,
        grid=(N_BIAS_HEADS,),
        in_specs=[pl.BlockSpec(memory_space=pltpu.SMEM),
                  pl.BlockSpec((None, BLOCK, 2 * BLOCK),
                               lambda h: (jnp.minimum(h // A_HEADS_PER_GROUP, groups), 0, 0))],
        out_specs=pl.BlockSpec((2, None, BLOCK, 2 * BLOCK), lambda h: (0, h, 0, 0)),
        out_shape=jax.ShapeDtypeStruct((2, N_BIAS_HEADS, BLOCK, 2 * BLOCK), F32),
        compiler_params=_cparams(("arbitrary",)),
        name="bias_tiles",
    )(rel_bias, _bucket_tiles())


ATT_SCALE = HEAD_DIM ** -0.5
HEADS_PER_STEP = 4
STEP_COLS = HEADS_PER_STEP * HEAD_DIM


def _scores(q, kp, kc, bias):
    k = jnp.concatenate([kp, kc], axis=0)
    s = lax.dot_general(q, k, (((1,), (1,)), ((), ())), preferred_element_type=F32)
    return s * ATT_SCALE + bias


def _attn_a_kernel(q_ref, kc_ref, kp_ref, vc_ref, vp_ref, bias_ref, o_ref, lse_ref, *, dilation, heads, qblocks):
    span = BLOCK * dilation
    first = jnp.minimum(pl.program_id(1), 1)
    for sb in range(qblocks):
        for r in range(dilation):
            def rows(blk):
                start = blk * span + r
                return pl.ds(start, BLOCK, stride=dilation) if dilation > 1 else pl.ds(start, BLOCK)

            for h in range(heads):
                sl = slice(h * HEAD_DIM, (h + 1) * HEAD_DIM)
                ld = lambda ref, blk: ref[rows(blk), sl].astype(BF16)
                prev = (lambda ref_p, ref_c: ld(ref_p, 0)) if sb == 0 else (lambda ref_p, ref_c: ld(ref_c, sb - 1))
                bias = bias_ref[first, h] if sb == 0 else bias_ref[1, h]
                s = _scores(ld(q_ref, sb), prev(kp_ref, kc_ref), ld(kc_ref, sb), bias)
                m = jnp.max(s, axis=-1, keepdims=True)
                p = jnp.exp(s - m)
                l = jnp.sum(p, axis=-1, keepdims=True)
                v = jnp.concatenate([prev(vp_ref, vc_ref), ld(vc_ref, sb)], axis=0)
                o = jnp.dot(p.astype(BF16), v, preferred_element_type=F32)
                o_ref[rows(sb), sl] = o / l
                lse_ref[rows(sb), sl] = jnp.broadcast_to(m + jnp.log(l), (BLOCK, HEAD_DIM))


def attn_dilated_group(pg, bias, gi, dilation, b, s):
    span = BLOCK * dilation
    qblocks = 4 if dilation < 16 else 1
    rows = span * qblocks
    heads = HEADS_PER_STEP if dilation == 1 else 1
    cols = heads * HEAD_DIM
    hs = HEADS_PER_STEP // heads
    pgv = pg.reshape(b, s, 3 * STEP_COLS)

    def cur(sec):
        return pl.BlockSpec((None, rows, cols), lambda bi, n, h: (bi, n, sec * hs + h))

    def prev(sec):
        return pl.BlockSpec((None, span, cols), lambda bi, n, h: (bi, jnp.maximum(n * qblocks - 1, 0), sec * hs + h))

    out_spec = pl.BlockSpec((None, rows, cols), lambda bi, n, h: (bi, n, h))
    out_sds = jax.ShapeDtypeStruct((b, s, STEP_COLS), F32)
    o, lse = pl.pallas_call(
        functools.partial(_attn_a_kernel, dilation=dilation, heads=heads, qblocks=qblocks),
        grid=(b, s // rows, hs),
        in_specs=[cur(0), cur(1), prev(1), cur(2), prev(2),
                  pl.BlockSpec((2, heads, BLOCK, 2 * BLOCK), lambda bi, n, h: (0, gi * hs + h, 0, 0))],
        out_specs=[out_spec, out_spec],
        out_shape=[out_sds, out_sds],
        compiler_params=_cparams(("parallel", "arbitrary", "arbitrary")),
        name=f"attn_dil{dilation}",
    )(pgv, pgv, pgv, pgv, pgv, bias)
    return o.reshape(b * s, STEP_COLS), lse.reshape(b * s, STEP_COLS)


def _lse_mix_kernel(o0_ref, o1_ref, o2_ref, l0_ref, l1_ref, l2_ref, o_ref):
    l0, l1, l2 = l0_ref[...], l1_ref[...], l2_ref[...]
    mx = jnp.maximum(jnp.maximum(l0, l1), l2)
    e0, e1, e2 = jnp.exp(l0 - mx), jnp.exp(l1 - mx), jnp.exp(l2 - mx)
    num = e0 * o0_ref[...] + e1 * o1_ref[...] + e2 * o2_ref[...]
    o_ref[...] = (num / (e0 + e1 + e2)).astype(o_ref.dtype)


def lse_mix(outs, lses, *, tr=1024):
    m, c = outs[0].shape
    tr = min(tr, m)
    spec = pl.BlockSpec((tr, c), lambda i: (i, 0))
    return pl.pallas_call(
        _lse_mix_kernel, grid=(m // tr,),
        in_specs=[spec] * 6, out_specs=spec,
        out_shape=jax.ShapeDtypeStruct((m, c), BF16),
        compiler_params=_cparams(("parallel",)), name="lse_mix",
    )(*outs, *lses)


SWA_QBLOCKS = 4


def _attn_c_kernel(sink_ref, q_ref, kc_ref, kp_ref, vc_ref, vp_ref, bias_ref, o_ref):
    kh = pl.program_id(2)
    first = jnp.minimum(pl.program_id(1), 1)
    for sb in range(SWA_QBLOCKS):
        rows = pl.ds(sb * BLOCK, BLOCK)
        before = pl.ds((sb - 1) * BLOCK, BLOCK)
        kprev, vprev = (kp_ref[...], vp_ref[...]) if sb == 0 else (kc_ref[before, :], vc_ref[before, :])
        v = jnp.concatenate([vprev, vc_ref[rows, :]], axis=0)
        for gq in range(HEADS_PER_STEP):
            sl = slice(gq * HEAD_DIM, (gq + 1) * HEAD_DIM)
            bias = bias_ref[first, gq] if sb == 0 else bias_ref[1, gq]
            s = _scores(q_ref[rows, sl], kprev, kc_ref[rows, :], bias)
            sink = sink_ref[kh * HEADS_PER_STEP + gq]
            m = jnp.maximum(jnp.max(s, axis=-1, keepdims=True), sink)
            w = jnp.exp(s - m)
            den = jnp.sum(w, axis=-1, keepdims=True) + jnp.exp(sink - m)
            o = jnp.dot(w.astype(BF16), v, preferred_element_type=F32)
            o_ref[rows, sl] = (o / den).astype(o_ref.dtype)


def attn_swa(pc, bias, sinks, b, s):
    rows = BLOCK * SWA_QBLOCKS
    pcv = pc.reshape(b, s, C_COLS)
    koff = C_Q_HEADS
    voff = C_Q_HEADS + C_KV_HEADS

    def cur(off):
        return pl.BlockSpec((None, rows, HEAD_DIM), lambda bi, n, kh: (bi, n, off + kh))

    def prev(off):
        return pl.BlockSpec((None, BLOCK, HEAD_DIM),
                            lambda bi, n, kh: (bi, jnp.maximum(n * SWA_QBLOCKS - 1, 0), off + kh))

    o = pl.pallas_call(
        _attn_c_kernel,
        grid=(b, s // rows, C_KV_HEADS),
        in_specs=[pl.BlockSpec(memory_space=pltpu.SMEM),
                  pl.BlockSpec((None, rows, STEP_COLS), lambda bi, n, kh: (bi, n, kh)),
                  cur(koff), prev(koff), cur(voff), prev(voff),
                  pl.BlockSpec((2, HEADS_PER_STEP, BLOCK, 2 * BLOCK),
                               lambda bi, n, kh: (0, A_HEADS // HEADS_PER_STEP + kh, 0, 0))],
        out_specs=pl.BlockSpec((None, rows, STEP_COLS), lambda bi, n, kh: (bi, n, kh)),
        out_shape=jax.ShapeDtypeStruct((b, s, C_OUT), BF16),
        compiler_params=_cparams(("parallel", "arbitrary", "arbitrary")),
        name="attn_swa",
    )(sinks, pcv, pcv, pcv, pcv, pcv, bias)
    return o.reshape(b * s, C_OUT)


B_QK = B_HEADS * B_DK


def _split3(x):
    hi = x.astype(BF16)
    r1 = x - hi.astype(F32)
    mid = r1.astype(BF16)
    lo = (r1 - mid.astype(F32)).astype(BF16)
    return hi, mid, lo


GLA_SUB = 4


def _gla_kernel(q_ref, k_ref, v_ref, r_ref, bg_ref, wg_ref, bgate_ref, gnorm_ref, o_ref, state_ref):
    @pl.when(pl.program_id(1) == 0)
    def _():
        state_ref[...] = jnp.zeros_like(state_ref)

    row = lax.broadcasted_iota(jnp.int32, (B_CHUNK, B_CHUNK), 0)
    coli = lax.broadcasted_iota(jnp.int32, (B_CHUNK, B_CHUNK), 1)
    causal = row >= coli
    tril = jnp.where(causal, 1.0, 0.0).astype(BF16)

    for sc in range(GLA_SUB):
        rows = pl.ds(sc * B_CHUNK, B_CHUNK)
        z = jnp.dot(bg_ref[rows, :].astype(BF16), wg_ref[...], preferred_element_type=F32) + bgate_ref[...]
        log_a = (jnp.minimum(z, 0.0) - jnp.log(1.0 + jnp.exp(-jnp.abs(z)))) / B_GATE_TAU
        cum = sum(jnp.dot(tril, part, preferred_element_type=F32) for part in _split3(log_a))
        last = cum[B_CHUNK - 1:B_CHUNK, :]

        qf = q_ref[rows, :].astype(F32) * (B_DK ** -0.5)
        kf = k_ref[rows, :].astype(F32)
        q_dec = (qf * jnp.exp(cum)).astype(BF16)
        k_inv = (kf * jnp.exp(-cum)).astype(BF16)
        k_out = (kf * jnp.exp(last - cum)).astype(BF16)
        decay = jnp.exp(last)

        for h in range(B_HEADS):
            ks = slice(h * B_DK, (h + 1) * B_DK)
            vs = slice(h * B_DV, (h + 1) * B_DV)
            qd, ki, ko, vh = q_dec[:, ks], k_inv[:, ks], k_out[:, ks], v_ref[rows, vs]
            att = lax.dot_general(qd, ki, (((1,), (1,)), ((), ())), preferred_element_type=F32)
            att = jnp.where(causal, att, 0.0).astype(BF16)
            st = state_ref[h]
            o = jnp.dot(att, vh, preferred_element_type=F32)
            o = o + lax.dot_general(qd, st.astype(BF16), (((1,), (1,)), ((), ())), preferred_element_type=F32)
            upd = lax.dot_general(vh, ko, (((0,), (0,)), ((), ())), preferred_element_type=F32)
            state_ref[h] = st * decay[:, ks] + upd
            on = _rms(o, gnorm_ref[:, vs])
            o_ref[rows, vs] = (on * jax.nn.silu(r_ref[rows, vs].astype(F32))).astype(o_ref.dtype)


def gla(pb, bg, w_gate, b_gate, gnorm, b, s):
    rows = B_CHUNK * GLA_SUB
    pbv = pb.reshape(b, s, B_COLS)
    bgv = bg.reshape(b, s, B_GATE_RANK)
    wg = w_gate.astype(BF16)
    full = lambda shape: pl.BlockSpec(shape, lambda bi, c: (0, 0))
    o = pl.pallas_call(
        _gla_kernel,
        grid=(b, s // rows),
        in_specs=[pl.BlockSpec((None, rows, B_QK), lambda bi, c: (bi, c, 0)),
                  pl.BlockSpec((None, rows, B_QK), lambda bi, c: (bi, c, 1)),
                  pl.BlockSpec((None, rows, B_OUT), lambda bi, c: (bi, c, 1)),
                  pl.BlockSpec((None, rows, B_OUT), lambda bi, c: (bi, c, 2)),
                  pl.BlockSpec((None, rows, B_GATE_RANK), lambda bi, c: (bi, c, 0)),
                  full((B_GATE_RANK, B_QK)), full((1, B_QK)), full((1, B_OUT))],
        out_specs=pl.BlockSpec((None, rows, B_OUT), lambda bi, c: (bi, c, 0)),
        out_shape=jax.ShapeDtypeStruct((b, s, B_OUT), BF16),
        scratch_shapes=[pltpu.VMEM((B_HEADS, B_DV, B_DK), F32)],
        compiler_params=_cparams(("parallel", "arbitrary")),
        name="gla",
    )(pbv, pbv, pbv, pbv, bgv, wg, b_gate.reshape(1, B_QK), jnp.tile(gnorm, B_HEADS).reshape(1, B_OUT))
    return o.reshape(b * s, B_OUT)


def token_mixers(pgs, pb, pc, bg, bias, w_gla_gate, b_gla_gate, gla_norm, attn_sinks, b, s):
    outs, lses = [], []
    for gi, (_, dilation) in enumerate(DIL_GROUPS):
        o, lse = attn_dilated_group(pgs[gi], bias, gi, dilation, b, s)
        outs.append(o)
        lses.append(lse)
    o_a = lse_mix(outs, lses)
    o_b = gla(pb, bg, w_gla_gate, b_gla_gate, gla_norm, b, s)
    o_c = attn_swa(pc, bias, attn_sinks, b, s)
    return o_a, o_b, o_c


def _cast_kernel(w_ref, o_ref):
    o_ref[...] = w_ref[...].astype(o_ref.dtype)


def cast_layer(w, l, *, tr=256):
    _, rows, cols = w.shape
    return pl.pallas_call(
        _cast_kernel, grid=(rows // tr,),
        in_specs=[pl.BlockSpec((None, tr, cols), lambda i: (l, i, 0))],
        out_specs=pl.BlockSpec((tr, cols), lambda i: (i, 0)),
        out_shape=jax.ShapeDtypeStruct((rows, cols), BF16),
        compiler_params=_cparams(("parallel",)), name="cast_w",
    )(w)


def _cast_halves_kernel(w_ref, lo_ref, hi_ref):
    half = lo_ref.shape[1]
    lo_ref[...] = w_ref[:, :half].astype(lo_ref.dtype)
    hi_ref[...] = w_ref[:, half:].astype(hi_ref.dtype)


def cast_halves(w, l, *, tr=64):
    _, rows, cols = w.shape
    half = cols // 2
    out = pl.BlockSpec((tr, half), lambda i: (i, 0))
    sds = jax.ShapeDtypeStruct((rows, half), BF16)
    return pl.pallas_call(
        _cast_halves_kernel, grid=(rows // tr,),
        in_specs=[pl.BlockSpec((None, tr, cols), lambda i: (l, i, 0))],
        out_specs=[out, out], out_shape=[sds, sds],
        compiler_params=_cparams(("parallel",)), name="cast_halves",
    )(w)


A_SEC = A_HEADS * HEAD_DIM


W_IN_STRIP = 128


def _cast_w_in_kernel(w_ref, a0_ref, a1_ref, a2_ref, b_ref, bg_ref, c_ref, g_ref):
    cvt = lambda lo, n: w_ref[lo:lo + n, :].astype(BF16)
    for gi, a_ref in enumerate((a0_ref, a1_ref, a2_ref)):
        for sec in range(3):
            a_ref[sec * STEP_COLS:(sec + 1) * STEP_COLS, :] = cvt(sec * A_SEC + gi * STEP_COLS, STEP_COLS)
    b_ref[...] = cvt(OFF_B, B_COLS)
    bg_ref[...] = cvt(OFF_BG, B_GATE_RANK)
    c_ref[...] = cvt(OFF_C, C_COLS)
    g_ref[...] = cvt(OFF_G, G_COLS)


def cast_w_in(w_in_t, l):
    _, n_in, d = w_in_t.shape
    heights = (3 * STEP_COLS,) * len(DIL_GROUPS) + (B_COLS, B_GATE_RANK, C_COLS, G_COLS)
    return pl.pallas_call(
        _cast_w_in_kernel, grid=(d // W_IN_STRIP,),
        in_specs=[pl.BlockSpec((None, n_in, W_IN_STRIP), lambda j: (l, 0, j))],
        out_specs=[pl.BlockSpec((n, W_IN_STRIP), lambda j: (0, j)) for n in heights],
        out_shape=[jax.ShapeDtypeStruct((n, d), BF16) for n in heights],
        compiler_params=_cparams(("parallel",)), name="cast_w_in",
    )(w_in_t)


def kernel(x, rel_bias, w_in, w_gla_gate, b_gla_gate, gla_norm, attn_sinks, w_br_a, w_br_b, w_br_c, w_out,
           g_pre_mix, g_post_mix, g_pre_ffn, g_post_ffn, w_up, conv_w, conv_b, w_down):
    b, s, d = x.shape
    depth = w_in.shape[0]
    xf = x.reshape(b * s, d)
    xn = rms_cast(xf, g_pre_mix[0])
    bias = bias_tiles(rel_bias)
    w_in_t = jnp.swapaxes(w_in, 1, 2)
    for l in range(depth):
        *w_a, w_b, w_bg, w_c, w_g = cast_w_in(w_in_t, l)
        proj = functools.partial(matmul, xn, trans_b=True)
        pgs = [proj(w_a[gi], tn=768, out_dtype=BF16 if dil == 1 else F32, name=f"proj_a{gi}")
               for gi, (_, dil) in enumerate(DIL_GROUPS)]
        pb = proj(w_b, name="proj_b")
        pc = proj(w_c, tn=768, name="proj_c")
        gates = proj(w_g, act="sigmoid", name="proj_g")
        bg = proj(w_bg, out_dtype=F32, name="proj_bg")

        o_a, o_b, o_c = token_mixers(pgs, pb, pc, bg, bias, w_gla_gate[l], b_gla_gate[l], gla_norm[l],
                                     attn_sinks[l], b, s)

        merged = merge(o_a, o_b, o_c, gates, cast_layer(w_br_a, l), cast_layer(w_br_b, l), cast_layer(w_br_c, l))
        y = matmul(merged, cast_layer(w_out, l), name="proj_out")
        xf, hn = norm_residual(y, xf, g_post_mix[l], g_pre_ffn[l])

        w_gate, w_upper = cast_halves(w_up, l)
        h = ffn_up(hn, w_gate, w_upper, conv_w[l][:, :D_FF], conv_w[l][:, D_FF:],
                   conv_b[l][:D_FF].reshape(1, D_FF), conv_b[l][D_FF:].reshape(1, D_FF), seq=s)
        f = matmul(h, cast_layer(w_down, l), tm=512, tn=512, name="ffn_down")
        xf, xn = norm_residual(f, xf, g_post_ffn[l], g_pre_mix[l + 1] if l + 1 < depth else None)
    return xf.reshape(b, s, d)
```

```python
import functools
import math

import jax
import jax.numpy as jnp
from jax import lax
from jax.experimental import pallas as pl
from jax.experimental.pallas import tpu as pltpu

F32 = jnp.float32
BF16 = jnp.bfloat16

D_MODEL = 4096
HEAD_DIM = 128
BLOCK = 128
NORM_EPS = 1e-6
DIL_GROUPS = ((128, 1), (512, 4), (2048, 16))
A_HEADS_PER_GROUP = 4
A_HEADS = 12
B_HEADS = 8
B_DK = 64
B_DV = 128
B_GATE_RANK = 16
B_GATE_TAU = 16.0
B_CHUNK = 64
B_OUT = 1024
C_Q_HEADS = 12
C_KV_HEADS = 3
C_WINDOW = 128
C_OUT = 1536
REL_BUCKETS = 32
REL_MAX_DIST = 2048
D_FF = 11008
CONV_W = 3

A_COLS = 3 * A_HEADS * HEAD_DIM
B_COLS = 2 * B_HEADS * B_DK + 2 * B_HEADS * B_DV
C_COLS = (C_Q_HEADS + 2 * C_KV_HEADS) * HEAD_DIM
G_COLS = 3 * D_MODEL
OFF_B = A_COLS
OFF_BG = OFF_B + B_COLS
OFF_C = OFF_BG + B_GATE_RANK
OFF_G = OFF_C + C_COLS


MXU_TILE_V7X = 256
VMEM_LIMIT = 56 * 1024 * 1024

TM = 1024
TN = 4 * MXU_TILE_V7X
TN_3 = 3 * MXU_TILE_V7X
TN_FFN = 2 * MXU_TILE_V7X
T_DOWN = 2 * MXU_TILE_V7X
TR_ROWWISE = 256


def _cparams(sem):
    return pltpu.CompilerParams(dimension_semantics=sem, vmem_limit_bytes=VMEM_LIMIT)


ROW_PARTS = 4


def _row_parts(tm):
    n = ROW_PARTS if tm % (ROW_PARTS * 128) == 0 else 1
    return [pl.ds(p * (tm // n), tm // n) for p in range(n)]


def _rms(x, g):
    ms = jnp.mean(x * x, axis=-1, keepdims=True)
    return x * lax.rsqrt(ms + NORM_EPS) * g


def _rms_cast_kernel(x_ref, g_ref, o_ref):
    o_ref[...] = _rms(x_ref[...], g_ref[...]).astype(o_ref.dtype)


def rms_cast(x, g, *, tr=TR_ROWWISE):
    m, d = x.shape
    return pl.pallas_call(
        _rms_cast_kernel,
        grid=(m // tr,),
        in_specs=[pl.BlockSpec((tr, d), lambda i: (i, 0)),
                  pl.BlockSpec((1, d), lambda i: (0, 0))],
        out_specs=pl.BlockSpec((tr, d), lambda i: (i, 0)),
        out_shape=jax.ShapeDtypeStruct((m, d), BF16),
        compiler_params=_cparams(("parallel",)),
        name="rms_cast",
    )(x, g.reshape(1, d))


def _norm_res_kernel(y_ref, x_ref, g_ref, g2_ref, xo_ref, hn_ref):
    xn = x_ref[...] + _rms(y_ref[...].astype(F32), g_ref[...])
    xo_ref[...] = xn
    hn_ref[...] = _rms(xn, g2_ref[...]).astype(hn_ref.dtype)


def _norm_res_last_kernel(y_ref, x_ref, g_ref, xo_ref):
    xo_ref[...] = x_ref[...] + _rms(y_ref[...].astype(F32), g_ref[...])


def norm_residual(y, x, g, g_next=None, *, tr=TR_ROWWISE):
    m, d = x.shape
    row = pl.BlockSpec((tr, d), lambda i: (i, 0))
    vec = pl.BlockSpec((1, d), lambda i: (0, 0))
    if g_next is None:
        return pl.pallas_call(
            _norm_res_last_kernel, grid=(m // tr,),
            in_specs=[row, row, vec], out_specs=row,
            out_shape=jax.ShapeDtypeStruct((m, d), F32),
            compiler_params=_cparams(("parallel",)), name="norm_res_last",
        )(y, x, g.reshape(1, d)), None
    return pl.pallas_call(
        _norm_res_kernel, grid=(m // tr,),
        in_specs=[row, row, vec, vec], out_specs=[row, row],
        out_shape=[jax.ShapeDtypeStruct((m, d), F32), jax.ShapeDtypeStruct((m, d), BF16)],
        compiler_params=_cparams(("parallel",)), name="norm_res",
    )(y, x, g.reshape(1, d), g_next.reshape(1, d))


def _mm_kernel(a_ref, w_ref, o_ref, *, act, trans_b):
    contract = (((1,), (1 if trans_b else 0,)), ((), ()))
    w = w_ref[...]
    for rows in _row_parts(a_ref.shape[0]):
        acc = lax.dot_general(a_ref[rows, :], w, contract, preferred_element_type=F32)
        if act == "sigmoid":
            acc = jax.nn.sigmoid(acc)
        o_ref[rows, :] = acc.astype(o_ref.dtype)


def matmul(a, w, *, trans_b=False, w_outer=False, tm=TM, tn=TN, out_dtype=BF16, act=None, name="mm"):
    m, k = a.shape
    n = w.shape[0 if trans_b else 1]
    tm = min(tm, m)
    tn = min(tn, n)
    assert m % tm == 0 and n % tn == 0
    ij = (lambda jo, ii: (ii, jo)) if w_outer else (lambda io, ji: (io, ji))
    a_map = lambda *g: (ij(*g)[0], 0)
    w_map = (lambda *g: (ij(*g)[1], 0)) if trans_b else (lambda *g: (0, ij(*g)[1]))
    return pl.pallas_call(
        functools.partial(_mm_kernel, act=act, trans_b=trans_b),
        grid=(n // tn, m // tm) if w_outer else (m // tm, n // tn),
        in_specs=[pl.BlockSpec((tm, k), a_map), pl.BlockSpec((tn, k) if trans_b else (k, tn), w_map)],
        out_specs=pl.BlockSpec((tm, tn), lambda *g: ij(*g)),
        out_shape=jax.ShapeDtypeStruct((m, n), out_dtype),
        compiler_params=_cparams(("parallel", "arbitrary")),
        name=name,
    )(a, w)


def _merge_kernel(oa_ref, ob_ref, oc_ref, ga_ref, gb_ref, gc_ref, wa_ref, wb_ref, wc_ref, o_ref):
    wa, wb, wc = wa_ref[...], wb_ref[...], wc_ref[...]
    for rows in _row_parts(o_ref.shape[0]):
        gated = lambda g_ref, o_in_ref, w: g_ref[rows, :].astype(F32) * jnp.dot(o_in_ref[rows, :], w,
                                                                                preferred_element_type=F32)
        acc = gated(ga_ref, oa_ref, wa)
        acc = acc + gated(gb_ref, ob_ref, wb)
        acc = acc + gated(gc_ref, oc_ref, wc)
        o_ref[rows, :] = acc.astype(o_ref.dtype)


def merge(o_a, o_b, o_c, gates, wa, wb, wc, *, tm=TM, tn=TN):
    m = o_a.shape[0]
    d = wa.shape[1]
    tm = min(tm, m)
    nj = d // tn
    act = lambda kdim: pl.BlockSpec((tm, kdim), lambda i, j: (i, 0))
    gate = lambda s: pl.BlockSpec((tm, tn), lambda i, j: (i, s * nj + j))
    wgt = lambda kdim: pl.BlockSpec((kdim, tn), lambda i, j: (0, j))
    return pl.pallas_call(
        _merge_kernel,
        grid=(m // tm, nj),
        in_specs=[act(o_a.shape[1]), act(o_b.shape[1]), act(o_c.shape[1]),
                  gate(0), gate(1), gate(2),
                  wgt(wa.shape[0]), wgt(wb.shape[0]), wgt(wc.shape[0])],
        out_specs=pl.BlockSpec((tm, tn), lambda i, j: (i, j)),
        out_shape=jax.ShapeDtypeStruct((m, d), BF16),
        compiler_params=_cparams(("parallel", "arbitrary")),
        name="merge",
    )(o_a, o_b, o_c, gates, gates, gates, wa, wb, wc)


TAIL = 8


def _ffn_up_kernel(hn_ref, wg_ref, wu_ref, cwg_ref, cwu_ref, cbg_ref, cbu_ref, o_ref, tail_g, tail_u, *,
                   tiles_per_seq):
    tm = o_ref.shape[0]
    j = pl.program_id(1)

    @pl.when(pl.program_id(0) % tiles_per_seq == 0)
    def _():
        tail_g[j] = jnp.zeros(tail_g.shape[1:], F32)
        tail_u[j] = jnp.zeros(tail_u.shape[1:], F32)

    a = hn_ref[...]

    def conv(w_ref, cw_ref, cb_ref, tail_ref):
        w = w_ref[...]
        part = tm // ROW_PARTS
        u = jnp.concatenate([jnp.dot(a[p * part:(p + 1) * part], w, preferred_element_type=F32)
                             for p in range(ROW_PARTS)], axis=0)
        ext = jnp.concatenate([tail_ref[j], u], axis=0)
        tail_ref[j] = u[tm - TAIL:]
        cw = cw_ref[...]
        y = cb_ref[...] + cw[0:1] * ext[TAIL - 2:TAIL - 2 + tm]
        y = y + cw[1:2] * ext[TAIL - 1:TAIL - 1 + tm]
        return y + cw[2:3] * ext[TAIL:TAIL + tm]

    gate = conv(wg_ref, cwg_ref, cbg_ref, tail_g)
    up = conv(wu_ref, cwu_ref, cbu_ref, tail_u)
    o_ref[...] = (jax.nn.silu(gate) * up).astype(o_ref.dtype)


def ffn_up(hn, w_gate, w_upper, cw_gate, cw_upper, cb_gate, cb_upper, *, seq, tm=TM, tn=TN_FFN):
    m, d = hn.shape
    ff = w_gate.shape[1]
    tm = min(tm, seq)
    assert seq % tm == 0
    nj = pl.cdiv(ff, tn)
    col = lambda rows: pl.BlockSpec((rows, tn), lambda i, j: (0, j))
    return pl.pallas_call(
        functools.partial(_ffn_up_kernel, tiles_per_seq=seq // tm),
        grid=(m // tm, nj),
        in_specs=[pl.BlockSpec((tm, d), lambda i, j: (i, 0)),
                  col(d), col(d), col(CONV_W), col(CONV_W), col(1), col(1)],
        out_specs=pl.BlockSpec((tm, tn), lambda i, j: (i, j)),
        out_shape=jax.ShapeDtypeStruct((m, ff), BF16),
        scratch_shapes=[pltpu.VMEM((nj, TAIL, tn), F32), pltpu.VMEM((nj, TAIL, tn), F32)],
        compiler_params=_cparams(("arbitrary", "arbitrary")),
        name="ffn_up",
    )(hn, w_gate, w_upper, cw_gate, cw_upper, cb_gate, cb_upper)


NEG = -1e30
N_BIAS_HEADS = A_HEADS + C_Q_HEADS


def _bucket_tiles():
    qi = jnp.arange(BLOCK)[:, None]
    kj = jnp.arange(2 * BLOCK)[None, :]
    rel = qi + BLOCK - kj
    max_exact = REL_BUCKETS // 2
    tiles = []
    for dilation, span in [(d, w // d) for w, d in DIL_GROUPS] + [(1, C_WINDOW - 1)]:
        dist = jnp.maximum(rel * dilation, 0)
        far = max_exact + (jnp.log(jnp.maximum(dist, 1).astype(F32) / max_exact)
                           / math.log(REL_MAX_DIST / max_exact) * (REL_BUCKETS - max_exact)).astype(jnp.int32)
        bucket = jnp.where(dist < max_exact, dist, jnp.minimum(far, REL_BUCKETS - 1))
        tiles.append(jnp.where((rel >= 0) & (rel <= span), bucket, -1))
    return jnp.stack(tiles).astype(jnp.int32)


def _bias_kernel(tab_ref, idx_ref, o_ref):
    h = pl.program_id(0)
    idx = idx_ref[...]
    acc = jnp.full(idx.shape, NEG, F32)
    for bkt in range(REL_BUCKETS):
        acc = jnp.where(idx == bkt, tab_ref[bkt, h], acc)
    col = lax.broadcasted_iota(jnp.int32, idx.shape, 1)
    o_ref[0] = jnp.where(col < BLOCK, NEG, acc)
    o_ref[1] = acc


def bias_tiles(rel_bias):
    groups = len(DIL_GROUPS)
    return pl.pallas_call(
        _bias_kernel,
        grid=(N_BIAS_HEADS,),
        in_specs=[pl.BlockSpec(memory_space=pltpu.SMEM),
                  pl.BlockSpec((None, BLOCK, 2 * BLOCK),
                               lambda h: (jnp.minimum(h // A_HEADS_PER_GROUP, groups), 0, 0))],
        out_specs=pl.BlockSpec((2, None, BLOCK, 2 * BLOCK), lambda h: (0, h, 0, 0)),
        out_shape=jax.ShapeDtypeStruct((2, N_BIAS_HEADS, BLOCK, 2 * BLOCK), F32),
        compiler_params=_cparams(("arbitrary",)),
        name="bias_tiles",
    )(rel_bias, _bucket_tiles())


ATT_SCALE = HEAD_DIM ** -0.5
HEADS_PER_STEP = 4
STEP_COLS = HEADS_PER_STEP * HEAD_DIM


def _scores(q, kp, kc, bias):
    k = jnp.concatenate([kp, kc], axis=0)
    s = lax.dot_general(q, k, (((1,), (1,)), ((), ())), preferred_element_type=F32)
    return s * ATT_SCALE + bias


def _attn_a_kernel(q_ref, kc_ref, kp_ref, vc_ref, vp_ref, bias_ref, o_ref, lse_ref, *, dilation, heads, qblocks):
    span = BLOCK * dilation
    first = jnp.minimum(pl.program_id(1), 1)
    for sb in range(qblocks):
        for r in range(dilation):
            def rows(blk):
                start = blk * span + r
                return pl.ds(start, BLOCK, stride=dilation) if dilation > 1 else pl.ds(start, BLOCK)

            for h in range(heads):
                sl = slice(h * HEAD_DIM, (h + 1) * HEAD_DIM)
                ld = lambda ref, blk: ref[rows(blk), sl].astype(BF16)
                prev = (lambda ref_p, ref_c: ld(ref_p, 0)) if sb == 0 else (lambda ref_p, ref_c: ld(ref_c, sb - 1))
                bias = bias_ref[first, h] if sb == 0 else bias_ref[1, h]
                s = _scores(ld(q_ref, sb), prev(kp_ref, kc_ref), ld(kc_ref, sb), bias)
                m = jnp.max(s, axis=-1, keepdims=True)
                p = jnp.exp(s - m)
                l = jnp.sum(p, axis=-1, keepdims=True)
                v = jnp.concatenate([prev(vp_ref, vc_ref), ld(vc_ref, sb)], axis=0)
                o = jnp.dot(p.astype(BF16), v, preferred_element_type=F32)
                o_ref[rows(sb), sl] = o / l
                lse_ref[rows(sb), sl] = jnp.broadcast_to(m + jnp.log(l), (BLOCK, HEAD_DIM))


def attn_dilated_group(pg, bias, gi, dilation, b, s):
    span = BLOCK * dilation
    qblocks = 4 if dilation < 16 else 1
    rows = span * qblocks
    heads = HEADS_PER_STEP if dilation == 1 else 1
    cols = heads * HEAD_DIM
    hs = HEADS_PER_STEP // heads
    pgv = pg.reshape(b, s, 3 * STEP_COLS)

    def cur(sec):
        return pl.BlockSpec((None, rows, cols), lambda bi, n, h: (bi, n, sec * hs + h))

    def prev(sec):
        return pl.BlockSpec((None, span, cols), lambda bi, n, h: (bi, jnp.maximum(n * qblocks - 1, 0), sec * hs + h))

    out_spec = pl.BlockSpec((None, rows, cols), lambda bi, n, h: (bi, n, h))
    out_sds = jax.ShapeDtypeStruct((b, s, STEP_COLS), F32)
    o, lse = pl.pallas_call(
        functools.partial(_attn_a_kernel, dilation=dilation, heads=heads, qblocks=qblocks),
        grid=(b, s // rows, hs),
        in_specs=[cur(0), cur(1), prev(1), cur(2), prev(2),
                  pl.BlockSpec((2, heads, BLOCK, 2 * BLOCK), lambda bi, n, h: (0, gi * hs + h, 0, 0))],
        out_specs=[out_spec, out_spec],
        out_shape=[out_sds, out_sds],
        compiler_params=_cparams(("parallel", "arbitrary", "arbitrary")),
        name=f"attn_dil{dilation}",
    )(pgv, pgv, pgv, pgv, pgv, bias)
    return o.reshape(b * s, STEP_COLS), lse.reshape(b * s, STEP_COLS)


def _lse_mix_kernel(o0_ref, o1_ref, o2_ref, l0_ref, l1_ref, l2_ref, o_ref):
    l0, l1, l2 = l0_ref[...], l1_ref[...], l2_ref[...]
    mx = jnp.maximum(jnp.maximum(l0, l1), l2)
    e0, e1, e2 = jnp.exp(l0 - mx), jnp.exp(l1 - mx), jnp.exp(l2 - mx)
    num = e0 * o0_ref[...] + e1 * o1_ref[...] + e2 * o2_ref[...]
    o_ref[...] = (num / (e0 + e1 + e2)).astype(o_ref.dtype)


def lse_mix(outs, lses, *, tr=4 * TR_ROWWISE):
    m, c = outs[0].shape
    tr = min(tr, m)
    spec = pl.BlockSpec((tr, c), lambda i: (i, 0))
    return pl.pallas_call(
        _lse_mix_kernel, grid=(m // tr,),
        in_specs=[spec] * 6, out_specs=spec,
        out_shape=jax.ShapeDtypeStruct((m, c), BF16),
        compiler_params=_cparams(("parallel",)), name="lse_mix",
    )(*outs, *lses)


SWA_QBLOCKS = 4


def _attn_c_kernel(sink_ref, q_ref, kc_ref, kp_ref, vc_ref, vp_ref, bias_ref, o_ref):
    kh = pl.program_id(2)
    first = jnp.minimum(pl.program_id(1), 1)
    for sb in range(SWA_QBLOCKS):
        rows = pl.ds(sb * BLOCK, BLOCK)
        before = pl.ds((sb - 1) * BLOCK, BLOCK)
        kprev, vprev = (kp_ref[...], vp_ref[...]) if sb == 0 else (kc_ref[before, :], vc_ref[before, :])
        v = jnp.concatenate([vprev, vc_ref[rows, :]], axis=0)
        for gq in range(HEADS_PER_STEP):
            sl = slice(gq * HEAD_DIM, (gq + 1) * HEAD_DIM)
            bias = bias_ref[first, gq] if sb == 0 else bias_ref[1, gq]
            s = _scores(q_ref[rows, sl], kprev, kc_ref[rows, :], bias)
            sink = sink_ref[kh * HEADS_PER_STEP + gq]
            m = jnp.maximum(jnp.max(s, axis=-1, keepdims=True), sink)
            w = jnp.exp(s - m)
            den = jnp.sum(w, axis=-1, keepdims=True) + jnp.exp(sink - m)
            o = jnp.dot(w.astype(BF16), v, preferred_element_type=F32)
            o_ref[rows, sl] = (o / den).astype(o_ref.dtype)


def attn_swa(pc, bias, sinks, b, s):
    rows = BLOCK * SWA_QBLOCKS
    pcv = pc.reshape(b, s, C_COLS)
    koff = C_Q_HEADS
    voff = C_Q_HEADS + C_KV_HEADS

    def cur(off):
        return pl.BlockSpec((None, rows, HEAD_DIM), lambda bi, n, kh: (bi, n, off + kh))

    def prev(off):
        return pl.BlockSpec((None, BLOCK, HEAD_DIM),
                            lambda bi, n, kh: (bi, jnp.maximum(n * SWA_QBLOCKS - 1, 0), off + kh))

    o = pl.pallas_call(
        _attn_c_kernel,
        grid=(b, s // rows, C_KV_HEADS),
        in_specs=[pl.BlockSpec(memory_space=pltpu.SMEM),
                  pl.BlockSpec((None, rows, STEP_COLS), lambda bi, n, kh: (bi, n, kh)),
                  cur(koff), prev(koff), cur(voff), prev(voff),
                  pl.BlockSpec((2, HEADS_PER_STEP, BLOCK, 2 * BLOCK),
                               lambda bi, n, kh: (0, A_HEADS // HEADS_PER_STEP + kh, 0, 0))],
        out_specs=pl.BlockSpec((None, rows, STEP_COLS), lambda bi, n, kh: (bi, n, kh)),
        out_shape=jax.ShapeDtypeStruct((b, s, C_OUT), BF16),
        compiler_params=_cparams(("parallel", "arbitrary", "arbitrary")),
        name="attn_swa",
    )(sinks, pcv, pcv, pcv, pcv, pcv, bias)
    return o.reshape(b * s, C_OUT)


B_QK = B_HEADS * B_DK


def _split3(x):
    hi = x.astype(BF16)
    r1 = x - hi.astype(F32)
    mid = r1.astype(BF16)
    lo = (r1 - mid.astype(F32)).astype(BF16)
    return hi, mid, lo


GLA_SUB = 4


def _gla_kernel(q_ref, k_ref, v_ref, r_ref, bg_ref, wg_ref, bgate_ref, gnorm_ref, o_ref, state_ref):
    @pl.when(pl.program_id(1) == 0)
    def _():
        state_ref[...] = jnp.zeros_like(state_ref)

    row = lax.broadcasted_iota(jnp.int32, (B_CHUNK, B_CHUNK), 0)
    coli = lax.broadcasted_iota(jnp.int32, (B_CHUNK, B_CHUNK), 1)
    causal = row >= coli
    tril = jnp.where(causal, 1.0, 0.0).astype(BF16)

    for sc in range(GLA_SUB):
        rows = pl.ds(sc * B_CHUNK, B_CHUNK)
        z = jnp.dot(bg_ref[rows, :].astype(BF16), wg_ref[...], preferred_element_type=F32) + bgate_ref[...]
        log_a = (jnp.minimum(z, 0.0) - jnp.log(1.0 + jnp.exp(-jnp.abs(z)))) / B_GATE_TAU
        cum = sum(jnp.dot(tril, part, preferred_element_type=F32) for part in _split3(log_a))
        last = cum[B_CHUNK - 1:B_CHUNK, :]

        qf = q_ref[rows, :].astype(F32) * (B_DK ** -0.5)
        kf = k_ref[rows, :].astype(F32)
        q_dec = (qf * jnp.exp(cum)).astype(BF16)
        k_inv = (kf * jnp.exp(-cum)).astype(BF16)
        k_out = (kf * jnp.exp(last - cum)).astype(BF16)
        decay = jnp.exp(last)

        for h in range(B_HEADS):
            ks = slice(h * B_DK, (h + 1) * B_DK)
            vs = slice(h * B_DV, (h + 1) * B_DV)
            qd, ki, ko, vh = q_dec[:, ks], k_inv[:, ks], k_out[:, ks], v_ref[rows, vs]
            att = lax.dot_general(qd, ki, (((1,), (1,)), ((), ())), preferred_element_type=F32)
            att = jnp.where(causal, att, 0.0).astype(BF16)
            st = state_ref[h]
            o = jnp.dot(att, vh, preferred_element_type=F32)
            o = o + lax.dot_general(qd, st.astype(BF16), (((1,), (1,)), ((), ())), preferred_element_type=F32)
            upd = lax.dot_general(vh, ko, (((0,), (0,)), ((), ())), preferred_element_type=F32)
            state_ref[h] = st * decay[:, ks] + upd
            on = _rms(o, gnorm_ref[:, vs])
            o_ref[rows, vs] = (on * jax.nn.silu(r_ref[rows, vs].astype(F32))).astype(o_ref.dtype)


def gla(pb, bg, w_gate, b_gate, gnorm, b, s):
    rows = B_CHUNK * GLA_SUB
    pbv = pb.reshape(b, s, B_COLS)
    bgv = bg.reshape(b, s, B_GATE_RANK)
    wg = w_gate.astype(BF16)
    full = lambda shape: pl.BlockSpec(shape, lambda bi, c: (0, 0))
    o = pl.pallas_call(
        _gla_kernel,
        grid=(b, s // rows),
        in_specs=[pl.BlockSpec((None, rows, B_QK), lambda bi, c: (bi, c, 0)),
                  pl.BlockSpec((None, rows, B_QK), lambda bi, c: (bi, c, 1)),
                  pl.BlockSpec((None, rows, B_OUT), lambda bi, c: (bi, c, 1)),
                  pl.BlockSpec((None, rows, B_OUT), lambda bi, c: (bi, c, 2)),
                  pl.BlockSpec((None, rows, B_GATE_RANK), lambda bi, c: (bi, c, 0)),
                  full((B_GATE_RANK, B_QK)), full((1, B_QK)), full((1, B_OUT))],
        out_specs=pl.BlockSpec((None, rows, B_OUT), lambda bi, c: (bi, c, 0)),
        out_shape=jax.ShapeDtypeStruct((b, s, B_OUT), BF16),
        scratch_shapes=[pltpu.VMEM((B_HEADS, B_DV, B_DK), F32)],
        compiler_params=_cparams(("parallel", "arbitrary")),
        name="gla",
    )(pbv, pbv, pbv, pbv, bgv, wg, b_gate.reshape(1, B_QK), jnp.tile(gnorm, B_HEADS).reshape(1, B_OUT))
    return o.reshape(b * s, B_OUT)


def token_mixers(pgs, pb, pc, bg, bias, w_gla_gate, b_gla_gate, gla_norm, attn_sinks, b, s):
    outs, lses = [], []
    for gi, (_, dilation) in enumerate(DIL_GROUPS):
        o, lse = attn_dilated_group(pgs[gi], bias, gi, dilation, b, s)
        outs.append(o)
        lses.append(lse)
    o_a = lse_mix(outs, lses)
    o_b = gla(pb, bg, w_gla_gate, b_gla_gate, gla_norm, b, s)
    o_c = attn_swa(pc, bias, attn_sinks, b, s)
    return o_a, o_b, o_c


def _cast_kernel(w_ref, o_ref):
    o_ref[...] = w_ref[...].astype(o_ref.dtype)


def cast_layer(w, l, *, tr=TR_ROWWISE):
    _, rows, cols = w.shape
    return pl.pallas_call(
        _cast_kernel, grid=(rows // tr,),
        in_specs=[pl.BlockSpec((None, tr, cols), lambda i: (l, i, 0))],
        out_specs=pl.BlockSpec((tr, cols), lambda i: (i, 0)),
        out_shape=jax.ShapeDtypeStruct((rows, cols), BF16),
        compiler_params=_cparams(("parallel",)), name="cast_w",
    )(w)


def _cast_halves_kernel(w_ref, lo_ref, hi_ref):
    half = lo_ref.shape[1]
    lo_ref[...] = w_ref[:, :half].astype(lo_ref.dtype)
    hi_ref[...] = w_ref[:, half:].astype(hi_ref.dtype)


def cast_halves(w, l, *, tr=64):
    _, rows, cols = w.shape
    half = cols // 2
    out = pl.BlockSpec((tr, half), lambda i: (i, 0))
    sds = jax.ShapeDtypeStruct((rows, half), BF16)
    return pl.pallas_call(
        _cast_halves_kernel, grid=(rows // tr,),
        in_specs=[pl.BlockSpec((None, tr, cols), lambda i: (l, i, 0))],
        out_specs=[out, out], out_shape=[sds, sds],
        compiler_params=_cparams(("parallel",)), name="cast_halves",
    )(w)


A_SEC = A_HEADS * HEAD_DIM


W_IN_STRIP = 128


def _cast_w_in_kernel(w_ref, a0_ref, a1_ref, a2_ref, b_ref, bg_ref, c_ref, g_ref):
    cvt = lambda lo, n: w_ref[lo:lo + n, :].astype(BF16)
    for gi, a_ref in enumerate((a0_ref, a1_ref, a2_ref)):
        for sec in range(3):
            a_ref[sec * STEP_COLS:(sec + 1) * STEP_COLS, :] = cvt(sec * A_SEC + gi * STEP_COLS, STEP_COLS)
    b_ref[...] = cvt(OFF_B, B_COLS)
    bg_ref[...] = cvt(OFF_BG, B_GATE_RANK)
    c_ref[...] = cvt(OFF_C, C_COLS)
    g_ref[...] = cvt(OFF_G, G_COLS)


def cast_w_in(w_in_t, l):
    _, n_in, d = w_in_t.shape
    heights = (3 * STEP_COLS,) * len(DIL_GROUPS) + (B_COLS, B_GATE_RANK, C_COLS, G_COLS)
    return pl.pallas_call(
        _cast_w_in_kernel, grid=(d // W_IN_STRIP,),
        in_specs=[pl.BlockSpec((None, n_in, W_IN_STRIP), lambda j: (l, 0, j))],
        out_specs=[pl.BlockSpec((n, W_IN_STRIP), lambda j: (0, j)) for n in heights],
        out_shape=[jax.ShapeDtypeStruct((n, d), BF16) for n in heights],
        compiler_params=_cparams(("parallel",)), name="cast_w_in",
    )(w_in_t)


def kernel(x, rel_bias, w_in, w_gla_gate, b_gla_gate, gla_norm, attn_sinks, w_br_a, w_br_b, w_br_c, w_out,
           g_pre_mix, g_post_mix, g_pre_ffn, g_post_ffn, w_up, conv_w, conv_b, w_down):
    b, s, d = x.shape
    depth = w_in.shape[0]
    xf = x.reshape(b * s, d)
    xn = rms_cast(xf, g_pre_mix[0])
    bias = bias_tiles(rel_bias)
    w_in_t = jnp.swapaxes(w_in, 1, 2)
    for l in range(depth):
        *w_a, w_b, w_bg, w_c, w_g = cast_w_in(w_in_t, l)
        proj = functools.partial(matmul, xn, trans_b=True)
        pgs = [proj(w_a[gi], tn=TN_3, out_dtype=BF16 if dil == 1 else F32, name=f"proj_a{gi}")
               for gi, (_, dil) in enumerate(DIL_GROUPS)]
        pb = proj(w_b, name="proj_b")
        pc = proj(w_c, tn=TN_3, name="proj_c")
        gates = proj(w_g, act="sigmoid", name="proj_g")
        bg = proj(w_bg, out_dtype=F32, name="proj_bg")

        o_a, o_b, o_c = token_mixers(pgs, pb, pc, bg, bias, w_gla_gate[l], b_gla_gate[l], gla_norm[l],
                                     attn_sinks[l], b, s)

        merged = merge(o_a, o_b, o_c, gates, cast_layer(w_br_a, l), cast_layer(w_br_b, l), cast_layer(w_br_c, l))
        y = matmul(merged, cast_layer(w_out, l), name="proj_out")
        xf, hn = norm_residual(y, xf, g_post_mix[l], g_pre_ffn[l])

        w_gate, w_upper = cast_halves(w_up, l)
        h = ffn_up(hn, w_gate, w_upper, conv_w[l][:, :D_FF], conv_w[l][:, D_FF:],
                   conv_b[l][:D_FF].reshape(1, D_FF), conv_b[l][D_FF:].reshape(1, D_FF), seq=s)
        f = matmul(h, cast_layer(w_down, l), w_outer=True, tm=T_DOWN, tn=T_DOWN, name="ffn_down")
        xf, xn = norm_residual(f, xf, g_post_ffn[l], g_pre_mix[l + 1] if l + 1 < depth else None)
    return xf.reshape(b, s, d)
```

```python
import functools
import math

import jax
import jax.numpy as jnp
from jax import lax
from jax.experimental import pallas as pl
from jax.experimental.pallas import tpu as pltpu

F32 = jnp.float32
BF16 = jnp.bfloat16

D_MODEL = 4096
HEAD_DIM = 128
BLOCK = 128
NORM_EPS = 1e-6
DIL_GROUPS = ((128, 1), (512, 4), (2048, 16))
A_HEADS_PER_GROUP = 4
A_HEADS = 12
B_HEADS = 8
B_DK = 64
B_DV = 128
B_GATE_RANK = 16
B_GATE_TAU = 16.0
B_CHUNK = 64
B_OUT = 1024
C_Q_HEADS = 12
C_KV_HEADS = 3
C_WINDOW = 128
C_OUT = 1536
REL_BUCKETS = 32
REL_MAX_DIST = 2048
D_FF = 11008
CONV_W = 3

A_COLS = 3 * A_HEADS * HEAD_DIM
B_COLS = 2 * B_HEADS * B_DK + 2 * B_HEADS * B_DV
C_COLS = (C_Q_HEADS + 2 * C_KV_HEADS) * HEAD_DIM
G_COLS = 3 * D_MODEL
OFF_B = A_COLS
OFF_BG = OFF_B + B_COLS
OFF_C = OFF_BG + B_GATE_RANK
OFF_G = OFF_C + C_COLS


MXU_TILE_V7X = 256
VMEM_LIMIT = 56 * 1024 * 1024

TM = 1024
TN = 4 * MXU_TILE_V7X
TN_3 = 3 * MXU_TILE_V7X
TN_FFN = 2 * MXU_TILE_V7X
T_DOWN = 2 * MXU_TILE_V7X
TR_ROWWISE = 256


def _cparams(sem):
    return pltpu.CompilerParams(dimension_semantics=sem, vmem_limit_bytes=VMEM_LIMIT)


ROW_PARTS = 4


def _row_parts(tm):
    n = ROW_PARTS if tm % (ROW_PARTS * 128) == 0 else 1
    return [pl.ds(p * (tm // n), tm // n) for p in range(n)]


def _rms(x, g):
    ms = jnp.mean(x * x, axis=-1, keepdims=True)
    return x * lax.rsqrt(ms + NORM_EPS) * g


def _rms_cast_kernel(x_ref, g_ref, o_ref):
    o_ref[...] = _rms(x_ref[...], g_ref[...]).astype(o_ref.dtype)


def rms_cast(x, g, *, tr=TR_ROWWISE):
    m, d = x.shape
    return pl.pallas_call(
        _rms_cast_kernel,
        grid=(m // tr,),
        in_specs=[pl.BlockSpec((tr, d), lambda i: (i, 0)),
                  pl.BlockSpec((1, d), lambda i: (0, 0))],
        out_specs=pl.BlockSpec((tr, d), lambda i: (i, 0)),
        out_shape=jax.ShapeDtypeStruct((m, d), BF16),
        compiler_params=_cparams(("parallel",)),
        name="rms_cast",
    )(x, g.reshape(1, d))


def _norm_res_kernel(y_ref, x_ref, g_ref, g2_ref, xo_ref, hn_ref):
    xn = x_ref[...] + _rms(y_ref[...].astype(F32), g_ref[...])
    xo_ref[...] = xn
    hn_ref[...] = _rms(xn, g2_ref[...]).astype(hn_ref.dtype)


def _norm_res_last_kernel(y_ref, x_ref, g_ref, xo_ref):
    xo_ref[...] = x_ref[...] + _rms(y_ref[...].astype(F32), g_ref[...])


def norm_residual(y, x, g, g_next=None, *, tr=TR_ROWWISE):
    m, d = x.shape
    row = pl.BlockSpec((tr, d), lambda i: (i, 0))
    vec = pl.BlockSpec((1, d), lambda i: (0, 0))
    if g_next is None:
        return pl.pallas_call(
            _norm_res_last_kernel, grid=(m // tr,),
            in_specs=[row, row, vec], out_specs=row,
            out_shape=jax.ShapeDtypeStruct((m, d), F32),
            compiler_params=_cparams(("parallel",)), name="norm_res_last",
        )(y, x, g.reshape(1, d)), None
    return pl.pallas_call(
        _norm_res_kernel, grid=(m // tr,),
        in_specs=[row, row, vec, vec], out_specs=[row, row],
        out_shape=[jax.ShapeDtypeStruct((m, d), F32), jax.ShapeDtypeStruct((m, d), BF16)],
        compiler_params=_cparams(("parallel",)), name="norm_res",
    )(y, x, g.reshape(1, d), g_next.reshape(1, d))


def _mm_kernel(a_ref, w_ref, o_ref, *, act, trans_b):
    contract = (((1,), (1 if trans_b else 0,)), ((), ()))
    w = w_ref[...]
    for rows in _row_parts(a_ref.shape[0]):
        acc = lax.dot_general(a_ref[rows, :], w, contract, preferred_element_type=F32)
        if act == "sigmoid":
            acc = jax.nn.sigmoid(acc)
        o_ref[rows, :] = acc.astype(o_ref.dtype)


def matmul(a, w, *, trans_b=False, w_outer=False, tm=TM, tn=TN, out_dtype=BF16, act=None, name="mm"):
    m, k = a.shape
    n = w.shape[0 if trans_b else 1]
    tm = min(tm, m)
    tn = min(tn, n)
    assert m % tm == 0 and n % tn == 0
    ij = (lambda jo, ii: (ii, jo)) if w_outer else (lambda io, ji: (io, ji))
    a_map = lambda *g: (ij(*g)[0], 0)
    w_map = (lambda *g: (ij(*g)[1], 0)) if trans_b else (lambda *g: (0, ij(*g)[1]))
    return pl.pallas_call(
        functools.partial(_mm_kernel, act=act, trans_b=trans_b),
        grid=(n // tn, m // tm) if w_outer else (m // tm, n // tn),
        in_specs=[pl.BlockSpec((tm, k), a_map), pl.BlockSpec((tn, k) if trans_b else (k, tn), w_map)],
        out_specs=pl.BlockSpec((tm, tn), lambda *g: ij(*g)),
        out_shape=jax.ShapeDtypeStruct((m, n), out_dtype),
        compiler_params=_cparams(("parallel", "arbitrary")),
        name=name,
    )(a, w)


def _merge_kernel(oa_ref, ob_ref, oc_ref, ga_ref, gb_ref, gc_ref, wa_ref, wb_ref, wc_ref, o_ref):
    wa, wb, wc = wa_ref[...], wb_ref[...], wc_ref[...]
    for rows in _row_parts(o_ref.shape[0]):
        gated = lambda g_ref, o_in_ref, w: g_ref[rows, :].astype(F32) * jnp.dot(o_in_ref[rows, :], w,
                                                                                preferred_element_type=F32)
        acc = gated(ga_ref, oa_ref, wa)
        acc = acc + gated(gb_ref, ob_ref, wb)
        acc = acc + gated(gc_ref, oc_ref, wc)
        o_ref[rows, :] = acc.astype(o_ref.dtype)


def merge(o_a, o_b, o_c, gates, wa, wb, wc, *, tm=TM, tn=TN):
    m = o_a.shape[0]
    d = wa.shape[1]
    tm = min(tm, m)
    nj = d // tn
    act = lambda kdim: pl.BlockSpec((tm, kdim), lambda i, j: (i, 0))
    gate = lambda s: pl.BlockSpec((tm, tn), lambda i, j: (i, s * nj + j))
    wgt = lambda kdim: pl.BlockSpec((kdim, tn), lambda i, j: (0, j))
    return pl.pallas_call(
        _merge_kernel,
        grid=(m // tm, nj),
        in_specs=[act(o_a.shape[1]), act(o_b.shape[1]), act(o_c.shape[1]),
                  gate(0), gate(1), gate(2),
                  wgt(wa.shape[0]), wgt(wb.shape[0]), wgt(wc.shape[0])],
        out_specs=pl.BlockSpec((tm, tn), lambda i, j: (i, j)),
        out_shape=jax.ShapeDtypeStruct((m, d), BF16),
        compiler_params=_cparams(("parallel", "arbitrary")),
        name="merge",
    )(o_a, o_b, o_c, gates, gates, gates, wa, wb, wc)


TAIL = 8


def _ffn_up_kernel(hn_ref, wg_ref, wu_ref, cwg_ref, cwu_ref, cbg_ref, cbu_ref, o_ref, tail_g, tail_u, *,
                   tiles_per_seq, last_cols):
    tm, tn = o_ref.shape
    j = pl.program_id(1)

    @pl.when(pl.program_id(0) % tiles_per_seq == 0)
    def _():
        tail_g[j] = jnp.zeros(tail_g.shape[1:], F32)
        tail_u[j] = jnp.zeros(tail_u.shape[1:], F32)

    def tile(cols):
        a = hn_ref[...]
        cs = slice(0, cols)
        part = tm // ROW_PARTS

        def dots(w):
            return jnp.concatenate([jnp.dot(a[p * part:(p + 1) * part], w, preferred_element_type=F32)
                                    for p in range(ROW_PARTS)], axis=0)

        if cols >= 2 * MXU_TILE_V7X:
            ug, uu = dots(wg_ref[:, cs]), dots(wu_ref[:, cs])
        else:
            u2 = dots(jnp.concatenate([wg_ref[:, cs], wu_ref[:, cs]], axis=1))
            ug, uu = u2[:, :cols], u2[:, cols:]

        def conv(u, cw_ref, cb_ref, tail_ref):
            ext = jnp.concatenate([tail_ref[j, :, cs], u], axis=0)
            tail_ref[j, :, cs] = u[tm - TAIL:]
            cw = cw_ref[:, cs]
            y = cb_ref[:, cs] + cw[0:1] * ext[TAIL - 2:TAIL - 2 + tm]
            y = y + cw[1:2] * ext[TAIL - 1:TAIL - 1 + tm]
            return y + cw[2:3] * ext[TAIL:TAIL + tm]

        gate = conv(ug, cwg_ref, cbg_ref, tail_g)
        up = conv(uu, cwu_ref, cbu_ref, tail_u)
        o_ref[:, cs] = (jax.nn.silu(gate) * up).astype(o_ref.dtype)

    if last_cols == tn:
        tile(tn)
    else:
        last = pl.num_programs(1) - 1
        pl.when(j < last)(lambda: tile(tn))
        pl.when(j == last)(lambda: tile(last_cols))


def ffn_up(hn, w_gate, w_upper, cw_gate, cw_upper, cb_gate, cb_upper, *, seq, tm=TM, tn=TN_FFN):
    m, d = hn.shape
    ff = w_gate.shape[1]
    tm = min(tm, seq)
    assert seq % tm == 0 and ff % 128 == 0
    nj = pl.cdiv(ff, tn)
    col = lambda rows: pl.BlockSpec((rows, tn), lambda i, j: (0, j))
    return pl.pallas_call(
        functools.partial(_ffn_up_kernel, tiles_per_seq=seq // tm, last_cols=ff - (nj - 1) * tn),
        grid=(m // tm, nj),
        in_specs=[pl.BlockSpec((tm, d), lambda i, j: (i, 0)),
                  col(d), col(d), col(CONV_W), col(CONV_W), col(1), col(1)],
        out_specs=pl.BlockSpec((tm, tn), lambda i, j: (i, j)),
        out_shape=jax.ShapeDtypeStruct((m, ff), BF16),
        scratch_shapes=[pltpu.VMEM((nj, TAIL, tn), F32), pltpu.VMEM((nj, TAIL, tn), F32)],
        compiler_params=_cparams(("arbitrary", "arbitrary")),
        name="ffn_up",
    )(hn, w_gate, w_upper, cw_gate, cw_upper, cb_gate, cb_upper)


NEG = -1e30
N_BIAS_HEADS = A_HEADS + C_Q_HEADS


def _bucket_tiles():
    qi = jnp.arange(BLOCK)[:, None]
    kj = jnp.arange(2 * BLOCK)[None, :]
    rel = qi + BLOCK - kj
    max_exact = REL_BUCKETS // 2
    tiles = []
    for dilation, span in [(d, w // d) for w, d in DIL_GROUPS] + [(1, C_WINDOW - 1)]:
        dist = jnp.maximum(rel * dilation, 0)
        far = max_exact + (jnp.log(jnp.maximum(dist, 1).astype(F32) / max_exact)
                           / math.log(REL_MAX_DIST / max_exact) * (REL_BUCKETS - max_exact)).astype(jnp.int32)
        bucket = jnp.where(dist < max_exact, dist, jnp.minimum(far, REL_BUCKETS - 1))
        tiles.append(jnp.where((rel >= 0) & (rel <= span), bucket, -1))
    return jnp.stack(tiles).astype(jnp.int32)


def _bias_kernel(tab_ref, idx_ref, o_ref):
    h = pl.program_id(0)
    idx = idx_ref[...]
    acc = jnp.full(idx.shape, NEG, F32)
    for bkt in range(REL_BUCKETS):
        acc = jnp.where(idx == bkt, tab_ref[bkt, h], acc)
    col = lax.broadcasted_iota(jnp.int32, idx.shape, 1)
    o_ref[0] = jnp.where(col < BLOCK, NEG, acc)
    o_ref[1] = acc


def bias_tiles(rel_bias):
    groups = len(DIL_GROUPS)
    return pl.pallas_call(
        _bias_kernel,
        grid=(N_BIAS_HEADS,),
        in_specs=[pl.BlockSpec(memory_space=pltpu.SMEM),
                  pl.BlockSpec((None, BLOCK, 2 * BLOCK),
                               lambda h: (jnp.minimum(h // A_HEADS_PER_GROUP, groups), 0, 0))],
        out_specs=pl.BlockSpec((2, None, BLOCK, 2 * BLOCK), lambda h: (0, h, 0, 0)),
        out_shape=jax.ShapeDtypeStruct((2, N_BIAS_HEADS, BLOCK, 2 * BLOCK), F32),
        compiler_params=_cparams(("arbitrary",)),
        name="bias_tiles",
    )(rel_bias, _bucket_tiles())


ATT_SCALE = HEAD_DIM ** -0.5
HEADS_PER_STEP = 4
STEP_COLS = HEADS_PER_STEP * HEAD_DIM


def _scores(q, kp, kc, bias):
    k = jnp.concatenate([kp, kc], axis=0)
    s = lax.dot_general(q, k, (((1,), (1,)), ((), ())), preferred_element_type=F32)
    return s * ATT_SCALE + bias


def _attn_a_kernel(q_ref, kc_ref, kp_ref, vc_ref, vp_ref, bias_ref, o_ref, lse_ref, *, dilation, heads, qblocks):
    span = BLOCK * dilation
    first = jnp.minimum(pl.program_id(1), 1)
    for sb in range(qblocks):
        for r in range(dilation):
            def rows(blk):
                start = blk * span + r
                return pl.ds(start, BLOCK, stride=dilation) if dilation > 1 else pl.ds(start, BLOCK)

            for h in range(heads):
                sl = slice(h * HEAD_DIM, (h + 1) * HEAD_DIM)
                ld = lambda ref, blk: ref[rows(blk), sl].astype(BF16)
                prev = (lambda ref_p, ref_c: ld(ref_p, 0)) if sb == 0 else (lambda ref_p, ref_c: ld(ref_c, sb - 1))
                bias = bias_ref[first, h] if sb == 0 else bias_ref[1, h]
                s = _scores(ld(q_ref, sb), prev(kp_ref, kc_ref), ld(kc_ref, sb), bias)
                m = jnp.max(s, axis=-1, keepdims=True)
                p = jnp.exp(s - m)
                l = jnp.sum(p, axis=-1, keepdims=True)
                v = jnp.concatenate([prev(vp_ref, vc_ref), ld(vc_ref, sb)], axis=0)
                o = jnp.dot(p.astype(BF16), v, preferred_element_type=F32)
                o_ref[rows(sb), sl] = o / l
                lse_ref[rows(sb), sl] = jnp.broadcast_to(m + jnp.log(l), (BLOCK, HEAD_DIM))


def attn_dilated_group(pg, bias, gi, dilation, b, s):
    span = BLOCK * dilation
    qblocks = 4 if dilation < 16 else 1
    rows = span * qblocks
    heads = HEADS_PER_STEP if dilation == 1 else 1
    cols = heads * HEAD_DIM
    hs = HEADS_PER_STEP // heads
    pgv = pg.reshape(b, s, 3 * STEP_COLS)

    def cur(sec):
        return pl.BlockSpec((None, rows, cols), lambda bi, n, h: (bi, n, sec * hs + h))

    def prev(sec):
        return pl.BlockSpec((None, span, cols), lambda bi, n, h: (bi, jnp.maximum(n * qblocks - 1, 0), sec * hs + h))

    out_spec = pl.BlockSpec((None, rows, cols), lambda bi, n, h: (bi, n, h))
    out_sds = jax.ShapeDtypeStruct((b, s, STEP_COLS), F32)
    o, lse = pl.pallas_call(
        functools.partial(_attn_a_kernel, dilation=dilation, heads=heads, qblocks=qblocks),
        grid=(b, s // rows, hs),
        in_specs=[cur(0), cur(1), prev(1), cur(2), prev(2),
                  pl.BlockSpec((2, heads, BLOCK, 2 * BLOCK), lambda bi, n, h: (0, gi * hs + h, 0, 0))],
        out_specs=[out_spec, out_spec],
        out_shape=[out_sds, out_sds],
        compiler_params=_cparams(("parallel", "arbitrary", "arbitrary")),
        name=f"attn_dil{dilation}",
    )(pgv, pgv, pgv, pgv, pgv, bias)
    return o.reshape(b * s, STEP_COLS), lse.reshape(b * s, STEP_COLS)


def _lse_mix_kernel(o0_ref, o1_ref, o2_ref, l0_ref, l1_ref, l2_ref, o_ref):
    l0, l1, l2 = l0_ref[...], l1_ref[...], l2_ref[...]
    mx = jnp.maximum(jnp.maximum(l0, l1), l2)
    e0, e1, e2 = jnp.exp(l0 - mx), jnp.exp(l1 - mx), jnp.exp(l2 - mx)
    num = e0 * o0_ref[...] + e1 * o1_ref[...] + e2 * o2_ref[...]
    o_ref[...] = (num / (e0 + e1 + e2)).astype(o_ref.dtype)


def lse_mix(outs, lses, *, tr=4 * TR_ROWWISE):
    m, c = outs[0].shape
    tr = min(tr, m)
    spec = pl.BlockSpec((tr, c), lambda i: (i, 0))
    return pl.pallas_call(
        _lse_mix_kernel, grid=(m // tr,),
        in_specs=[spec] * 6, out_specs=spec,
        out_shape=jax.ShapeDtypeStruct((m, c), BF16),
        compiler_params=_cparams(("parallel",)), name="lse_mix",
    )(*outs, *lses)


SWA_QBLOCKS = 4


def _attn_c_kernel(sink_ref, q_ref, kc_ref, kp_ref, vc_ref, vp_ref, bias_ref, o_ref):
    kh = pl.program_id(2)
    first = jnp.minimum(pl.program_id(1), 1)
    for sb in range(SWA_QBLOCKS):
        rows = pl.ds(sb * BLOCK, BLOCK)
        before = pl.ds((sb - 1) * BLOCK, BLOCK)
        kprev, vprev = (kp_ref[...], vp_ref[...]) if sb == 0 else (kc_ref[before, :], vc_ref[before, :])
        v = jnp.concatenate([vprev, vc_ref[rows, :]], axis=0)
        for gq in range(HEADS_PER_STEP):
            sl = slice(gq * HEAD_DIM, (gq + 1) * HEAD_DIM)
            bias = bias_ref[first, gq] if sb == 0 else bias_ref[1, gq]
            s = _scores(q_ref[rows, sl], kprev, kc_ref[rows, :], bias)
            sink = sink_ref[kh * HEADS_PER_STEP + gq]
            m = jnp.maximum(jnp.max(s, axis=-1, keepdims=True), sink)
            w = jnp.exp(s - m)
            den = jnp.sum(w, axis=-1, keepdims=True) + jnp.exp(sink - m)
            o = jnp.dot(w.astype(BF16), v, preferred_element_type=F32)
            o_ref[rows, sl] = (o / den).astype(o_ref.dtype)


def attn_swa(pc, bias, sinks, b, s):
    rows = BLOCK * SWA_QBLOCKS
    pcv = pc.reshape(b, s, C_COLS)
    koff = C_Q_HEADS
    voff = C_Q_HEADS + C_KV_HEADS

    def cur(off):
        return pl.BlockSpec((None, rows, HEAD_DIM), lambda bi, n, kh: (bi, n, off + kh))

    def prev(off):
        return pl.BlockSpec((None, BLOCK, HEAD_DIM),
                            lambda bi, n, kh: (bi, jnp.maximum(n * SWA_QBLOCKS - 1, 0), off + kh))

    o = pl.pallas_call(
        _attn_c_kernel,
        grid=(b, s // rows, C_KV_HEADS),
        in_specs=[pl.BlockSpec(memory_space=pltpu.SMEM),
                  pl.BlockSpec((None, rows, STEP_COLS), lambda bi, n, kh: (bi, n, kh)),
                  cur(koff), prev(koff), cur(voff), prev(voff),
                  pl.BlockSpec((2, HEADS_PER_STEP, BLOCK, 2 * BLOCK),
                               lambda bi, n, kh: (0, A_HEADS // HEADS_PER_STEP + kh, 0, 0))],
        out_specs=pl.BlockSpec((None, rows, STEP_COLS), lambda bi, n, kh: (bi, n, kh)),
        out_shape=jax.ShapeDtypeStruct((b, s, C_OUT), BF16),
        compiler_params=_cparams(("parallel", "arbitrary", "arbitrary")),
        name="attn_swa",
    )(sinks, pcv, pcv, pcv, pcv, pcv, bias)
    return o.reshape(b * s, C_OUT)


B_QK = B_HEADS * B_DK


def _split3(x):
    hi = x.astype(BF16)
    r1 = x - hi.astype(F32)
    mid = r1.astype(BF16)
    lo = (r1 - mid.astype(F32)).astype(BF16)
    return hi, mid, lo


GLA_SUB = 4


def _gla_kernel(q_ref, k_ref, v_ref, r_ref, bg_ref, wg_ref, bgate_ref, gnorm_ref, o_ref, state_ref):
    @pl.when(pl.program_id(1) == 0)
    def _():
        state_ref[...] = jnp.zeros_like(state_ref)

    row = lax.broadcasted_iota(jnp.int32, (B_CHUNK, B_CHUNK), 0)
    coli = lax.broadcasted_iota(jnp.int32, (B_CHUNK, B_CHUNK), 1)
    causal = row >= coli
    tril = jnp.where(causal, 1.0, 0.0).astype(BF16)

    for sc in range(GLA_SUB):
        rows = pl.ds(sc * B_CHUNK, B_CHUNK)
        z = jnp.dot(bg_ref[rows, :].astype(BF16), wg_ref[...], preferred_element_type=F32) + bgate_ref[...]
        log_a = (jnp.minimum(z, 0.0) - jnp.log(1.0 + jnp.exp(-jnp.abs(z)))) / B_GATE_TAU
        cum = sum(jnp.dot(tril, part, preferred_element_type=F32) for part in _split3(log_a))
        last = cum[B_CHUNK - 1:B_CHUNK, :]

        qf = q_ref[rows, :].astype(F32) * (B_DK ** -0.5)
        kf = k_ref[rows, :].astype(F32)
        q_dec = (qf * jnp.exp(cum)).astype(BF16)
        k_inv = (kf * jnp.exp(-cum)).astype(BF16)
        k_out = (kf * jnp.exp(last - cum)).astype(BF16)
        decay = jnp.exp(last)

        for h in range(B_HEADS):
            ks = slice(h * B_DK, (h + 1) * B_DK)
            vs = slice(h * B_DV, (h + 1) * B_DV)
            qd, ki, ko, vh = q_dec[:, ks], k_inv[:, ks], k_out[:, ks], v_ref[rows, vs]
            att = lax.dot_general(qd, ki, (((1,), (1,)), ((), ())), preferred_element_type=F32)
            att = jnp.where(causal, att, 0.0).astype(BF16)
            st = state_ref[h]
            o = jnp.dot(att, vh, preferred_element_type=F32)
            o = o + lax.dot_general(qd, st.astype(BF16), (((1,), (1,)), ((), ())), preferred_element_type=F32)
            upd = lax.dot_general(vh, ko, (((0,), (0,)), ((), ())), preferred_element_type=F32)
            state_ref[h] = st * decay[:, ks] + upd
            on = _rms(o, gnorm_ref[:, vs])
            o_ref[rows, vs] = (on * jax.nn.silu(r_ref[rows, vs].astype(F32))).astype(o_ref.dtype)


def gla(pb, bg, w_gate, b_gate, gnorm, b, s):
    rows = B_CHUNK * GLA_SUB
    pbv = pb.reshape(b, s, B_COLS)
    bgv = bg.reshape(b, s, B_GATE_RANK)
    wg = w_gate.astype(BF16)
    full = lambda shape: pl.BlockSpec(shape, lambda bi, c: (0, 0))
    o = pl.pallas_call(
        _gla_kernel,
        grid=(b, s // rows),
        in_specs=[pl.BlockSpec((None, rows, B_QK), lambda bi, c: (bi, c, 0)),
                  pl.BlockSpec((None, rows, B_QK), lambda bi, c: (bi, c, 1)),
                  pl.BlockSpec((None, rows, B_OUT), lambda bi, c: (bi, c, 1)),
                  pl.BlockSpec((None, rows, B_OUT), lambda bi, c: (bi, c, 2)),
                  pl.BlockSpec((None, rows, B_GATE_RANK), lambda bi, c: (bi, c, 0)),
                  full((B_GATE_RANK, B_QK)), full((1, B_QK)), full((1, B_OUT))],
        out_specs=pl.BlockSpec((None, rows, B_OUT), lambda bi, c: (bi, c, 0)),
        out_shape=jax.ShapeDtypeStruct((b, s, B_OUT), BF16),
        scratch_shapes=[pltpu.VMEM((B_HEADS, B_DV, B_DK), F32)],
        compiler_params=_cparams(("parallel", "arbitrary")),
        name="gla",
    )(pbv, pbv, pbv, pbv, bgv, wg, b_gate.reshape(1, B_QK), jnp.tile(gnorm, B_HEADS).reshape(1, B_OUT))
    return o.reshape(b * s, B_OUT)


def token_mixers(pgs, pb, pc, bg, bias, w_gla_gate, b_gla_gate, gla_norm, attn_sinks, b, s):
    outs, lses = [], []
    for gi, (_, dilation) in enumerate(DIL_GROUPS):
        o, lse = attn_dilated_group(pgs[gi], bias, gi, dilation, b, s)
        outs.append(o)
        lses.append(lse)
    o_a = lse_mix(outs, lses)
    o_b = gla(pb, bg, w_gla_gate, b_gla_gate, gla_norm, b, s)
    o_c = attn_swa(pc, bias, attn_sinks, b, s)
    return o_a, o_b, o_c


def _cast_kernel(w_ref, o_ref):
    o_ref[...] = w_ref[...].astype(o_ref.dtype)


def cast_layer(w, l, *, tr=TR_ROWWISE):
    _, rows, cols = w.shape
    return pl.pallas_call(
        _cast_kernel, grid=(rows // tr,),
        in_specs=[pl.BlockSpec((None, tr, cols), lambda i: (l, i, 0))],
        out_specs=pl.BlockSpec((tr, cols), lambda i: (i, 0)),
        out_shape=jax.ShapeDtypeStruct((rows, cols), BF16),
        compiler_params=_cparams(("parallel",)), name="cast_w",
    )(w)


def _cast_halves_kernel(w_ref, lo_ref, hi_ref):
    half = lo_ref.shape[1]
    lo_ref[...] = w_ref[:, :half].astype(lo_ref.dtype)
    hi_ref[...] = w_ref[:, half:].astype(hi_ref.dtype)


def cast_halves(w, l, *, tr=64):
    _, rows, cols = w.shape
    half = cols // 2
    out = pl.BlockSpec((tr, half), lambda i: (i, 0))
    sds = jax.ShapeDtypeStruct((rows, half), BF16)
    return pl.pallas_call(
        _cast_halves_kernel, grid=(rows // tr,),
        in_specs=[pl.BlockSpec((None, tr, cols), lambda i: (l, i, 0))],
        out_specs=[out, out], out_shape=[sds, sds],
        compiler_params=_cparams(("parallel",)), name="cast_halves",
    )(w)


A_SEC = A_HEADS * HEAD_DIM


W_IN_STRIP = 128


def _cast_w_in_kernel(w_ref, a0_ref, a1_ref, a2_ref, b_ref, bg_ref, c_ref, g_ref):
    cvt = lambda lo, n: w_ref[lo:lo + n, :].astype(BF16)
    for gi, a_ref in enumerate((a0_ref, a1_ref, a2_ref)):
        for sec in range(3):
            a_ref[sec * STEP_COLS:(sec + 1) * STEP_COLS, :] = cvt(sec * A_SEC + gi * STEP_COLS, STEP_COLS)
    b_ref[...] = cvt(OFF_B, B_COLS)
    bg_ref[...] = cvt(OFF_BG, B_GATE_RANK)
    c_ref[...] = cvt(OFF_C, C_COLS)
    g_ref[...] = cvt(OFF_G, G_COLS)


def cast_w_in(w_in_t, l):
    _, n_in, d = w_in_t.shape
    heights = (3 * STEP_COLS,) * len(DIL_GROUPS) + (B_COLS, B_GATE_RANK, C_COLS, G_COLS)
    return pl.pallas_call(
        _cast_w_in_kernel, grid=(d // W_IN_STRIP,),
        in_specs=[pl.BlockSpec((None, n_in, W_IN_STRIP), lambda j: (l, 0, j))],
        out_specs=[pl.BlockSpec((n, W_IN_STRIP), lambda j: (0, j)) for n in heights],
        out_shape=[jax.ShapeDtypeStruct((n, d), BF16) for n in heights],
        compiler_params=_cparams(("parallel",)), name="cast_w_in",
    )(w_in_t)


def kernel(x, rel_bias, w_in, w_gla_gate, b_gla_gate, gla_norm, attn_sinks, w_br_a, w_br_b, w_br_c, w_out,
           g_pre_mix, g_post_mix, g_pre_ffn, g_post_ffn, w_up, conv_w, conv_b, w_down):
    b, s, d = x.shape
    depth = w_in.shape[0]
    xf = x.reshape(b * s, d)
    xn = rms_cast(xf, g_pre_mix[0])
    bias = bias_tiles(rel_bias)
    w_in_t = jnp.swapaxes(w_in, 1, 2)
    for l in range(depth):
        *w_a, w_b, w_bg, w_c, w_g = cast_w_in(w_in_t, l)
        proj = functools.partial(matmul, xn, trans_b=True)
        pgs = [proj(w_a[gi], tn=TN_3, out_dtype=BF16 if dil == 1 else F32, name=f"proj_a{gi}")
               for gi, (_, dil) in enumerate(DIL_GROUPS)]
        pb = proj(w_b, name="proj_b")
        pc = proj(w_c, tn=TN_3, name="proj_c")
        gates = proj(w_g, act="sigmoid", name="proj_g")
        bg = proj(w_bg, out_dtype=F32, name="proj_bg")

        o_a, o_b, o_c = token_mixers(pgs, pb, pc, bg, bias, w_gla_gate[l], b_gla_gate[l], gla_norm[l],
                                     attn_sinks[l], b, s)

        merged = merge(o_a, o_b, o_c, gates, cast_layer(w_br_a, l), cast_layer(w_br_b, l), cast_layer(w_br_c, l))
        y = matmul(merged, cast_layer(w_out, l), name="proj_out")
        xf, hn = norm_residual(y, xf, g_post_mix[l], g_pre_ffn[l])

        w_gate, w_upper = cast_halves(w_up, l)
        h = ffn_up(hn, w_gate, w_upper, conv_w[l][:, :D_FF], conv_w[l][:, D_FF:],
                   conv_b[l][:D_FF].reshape(1, D_FF), conv_b[l][D_FF:].reshape(1, D_FF), seq=s)
        f = matmul(h, cast_layer(w_down, l), w_outer=True, tm=T_DOWN, tn=T_DOWN, name="ffn_down")
        xf, xn = norm_residual(f, xf, g_post_ffn[l], g_pre_mix[l + 1] if l + 1 < depth else None)
    return xf.reshape(b, s, d)
```

```python
import functools
import math

import jax
import jax.numpy as jnp
from jax import lax
from jax.experimental import pallas as pl
from jax.experimental.pallas import tpu as pltpu

F32 = jnp.float32
BF16 = jnp.bfloat16

D_MODEL = 4096
HEAD_DIM = 128
BLOCK = 128
NORM_EPS = 1e-6
DIL_GROUPS = ((128, 1), (512, 4), (2048, 16))
A_HEADS_PER_GROUP = 4
A_HEADS = 12
B_HEADS = 8
B_DK = 64
B_DV = 128
B_GATE_RANK = 16
B_GATE_TAU = 16.0
B_CHUNK = 64
B_OUT = 1024
C_Q_HEADS = 12
C_KV_HEADS = 3
C_WINDOW = 128
C_OUT = 1536
REL_BUCKETS = 32
REL_MAX_DIST = 2048
D_FF = 11008
CONV_W = 3

A_COLS = 3 * A_HEADS * HEAD_DIM
B_COLS = 2 * B_HEADS * B_DK + 2 * B_HEADS * B_DV
C_COLS = (C_Q_HEADS + 2 * C_KV_HEADS) * HEAD_DIM
G_COLS = 3 * D_MODEL
OFF_B = A_COLS
OFF_BG = OFF_B + B_COLS
OFF_C = OFF_BG + B_GATE_RANK
OFF_G = OFF_C + C_COLS


MXU_TILE_V7X = 256
VMEM_LIMIT = 56 * 1024 * 1024

TM = 1024
TN = 4 * MXU_TILE_V7X
TN_3 = 3 * MXU_TILE_V7X
TN_FFN = 2 * MXU_TILE_V7X
T_DOWN = 2 * MXU_TILE_V7X
TR_ROWWISE = 256


def _cparams(sem):
    return pltpu.CompilerParams(dimension_semantics=sem, vmem_limit_bytes=VMEM_LIMIT)


ROW_PARTS = 4


def _row_parts(tm):
    n = ROW_PARTS if tm % (ROW_PARTS * 128) == 0 else 1
    return [pl.ds(p * (tm // n), tm // n) for p in range(n)]


def _rms(x, g):
    ms = jnp.mean(x * x, axis=-1, keepdims=True)
    return x * lax.rsqrt(ms + NORM_EPS) * g


def _rms_cast_kernel(x_ref, g_ref, o_ref):
    o_ref[...] = _rms(x_ref[...], g_ref[...]).astype(o_ref.dtype)


def rms_cast(x, g, *, tr=TR_ROWWISE):
    m, d = x.shape
    return pl.pallas_call(
        _rms_cast_kernel,
        grid=(m // tr,),
        in_specs=[pl.BlockSpec((tr, d), lambda i: (i, 0)),
                  pl.BlockSpec((1, d), lambda i: (0, 0))],
        out_specs=pl.BlockSpec((tr, d), lambda i: (i, 0)),
        out_shape=jax.ShapeDtypeStruct((m, d), BF16),
        compiler_params=_cparams(("parallel",)),
        name="rms_cast",
    )(x, g.reshape(1, d))


def _norm_res_kernel(y_ref, x_ref, g_ref, g2_ref, xo_ref, hn_ref):
    xn = x_ref[...] + _rms(y_ref[...].astype(F32), g_ref[...])
    xo_ref[...] = xn
    hn_ref[...] = _rms(xn, g2_ref[...]).astype(hn_ref.dtype)


def _norm_res_last_kernel(y_ref, x_ref, g_ref, xo_ref):
    xo_ref[...] = x_ref[...] + _rms(y_ref[...].astype(F32), g_ref[...])


def norm_residual(y, x, g, g_next=None, *, tr=TR_ROWWISE):
    m, d = x.shape
    row = pl.BlockSpec((tr, d), lambda i: (i, 0))
    vec = pl.BlockSpec((1, d), lambda i: (0, 0))
    if g_next is None:
        return pl.pallas_call(
            _norm_res_last_kernel, grid=(m // tr,),
            in_specs=[row, row, vec], out_specs=row,
            out_shape=jax.ShapeDtypeStruct((m, d), F32),
            compiler_params=_cparams(("parallel",)), name="norm_res_last",
        )(y, x, g.reshape(1, d)), None
    return pl.pallas_call(
        _norm_res_kernel, grid=(m // tr,),
        in_specs=[row, row, vec, vec], out_specs=[row, row],
        out_shape=[jax.ShapeDtypeStruct((m, d), F32), jax.ShapeDtypeStruct((m, d), BF16)],
        compiler_params=_cparams(("parallel",)), name="norm_res",
    )(y, x, g.reshape(1, d), g_next.reshape(1, d))


def _mm_kernel(a_ref, w_ref, o_ref, *, act, trans_b):
    contract = (((1,), (1 if trans_b else 0,)), ((), ()))
    w = w_ref[...]
    for rows in _row_parts(a_ref.shape[0]):
        acc = lax.dot_general(a_ref[rows, :], w, contract, preferred_element_type=F32)
        if act == "sigmoid":
            acc = jax.nn.sigmoid(acc)
        o_ref[rows, :] = acc.astype(o_ref.dtype)


def matmul(a, w, *, trans_b=False, w_outer=False, tm=TM, tn=TN, out_dtype=BF16, act=None, name="mm"):
    m, k = a.shape
    n = w.shape[0 if trans_b else 1]
    tm = min(tm, m)
    tn = min(tn, n)
    assert m % tm == 0 and n % tn == 0
    ij = (lambda jo, ii: (ii, jo)) if w_outer else (lambda io, ji: (io, ji))
    a_map = lambda *g: (ij(*g)[0], 0)
    w_map = (lambda *g: (ij(*g)[1], 0)) if trans_b else (lambda *g: (0, ij(*g)[1]))
    return pl.pallas_call(
        functools.partial(_mm_kernel, act=act, trans_b=trans_b),
        grid=(n // tn, m // tm) if w_outer else (m // tm, n // tn),
        in_specs=[pl.BlockSpec((tm, k), a_map), pl.BlockSpec((tn, k) if trans_b else (k, tn), w_map)],
        out_specs=pl.BlockSpec((tm, tn), lambda *g: ij(*g)),
        out_shape=jax.ShapeDtypeStruct((m, n), out_dtype),
        compiler_params=_cparams(("parallel", "arbitrary")),
        name=name,
    )(a, w)


def _merge_kernel(oa_ref, ob_ref, oc_ref, ga_ref, gb_ref, gc_ref, wa_ref, wb_ref, wc_ref, o_ref):
    wa, wb, wc = wa_ref[...], wb_ref[...], wc_ref[...]
    for rows in _row_parts(o_ref.shape[0]):
        gated = lambda g_ref, o_in_ref, w: g_ref[rows, :].astype(F32) * jnp.dot(o_in_ref[rows, :], w,
                                                                                preferred_element_type=F32)
        acc = gated(ga_ref, oa_ref, wa)
        acc = acc + gated(gb_ref, ob_ref, wb)
        acc = acc + gated(gc_ref, oc_ref, wc)
        o_ref[rows, :] = acc.astype(o_ref.dtype)


def merge(o_a, o_b, o_c, gates, wa, wb, wc, *, tm=TM, tn=TN):
    m = o_a.shape[0]
    d = wa.shape[1]
    tm = min(tm, m)
    nj = d // tn
    act = lambda kdim: pl.BlockSpec((tm, kdim), lambda i, j: (i, 0))
    gate = lambda s: pl.BlockSpec((tm, tn), lambda i, j: (i, s * nj + j))
    wgt = lambda kdim: pl.BlockSpec((kdim, tn), lambda i, j: (0, j))
    return pl.pallas_call(
        _merge_kernel,
        grid=(m // tm, nj),
        in_specs=[act(o_a.shape[1]), act(o_b.shape[1]), act(o_c.shape[1]),
                  gate(0), gate(1), gate(2),
                  wgt(wa.shape[0]), wgt(wb.shape[0]), wgt(wc.shape[0])],
        out_specs=pl.BlockSpec((tm, tn), lambda i, j: (i, j)),
        out_shape=jax.ShapeDtypeStruct((m, d), BF16),
        compiler_params=_cparams(("parallel", "arbitrary")),
        name="merge",
    )(o_a, o_b, o_c, gates, gates, gates, wa, wb, wc)


TAIL = 8


def _ffn_up_kernel(hn_ref, wg_ref, wu_ref, cwg_ref, cwu_ref, cbg_ref, cbu_ref, o_ref, tail_g, tail_u, *,
                   tiles_per_seq, last_cols):
    tm, tn = o_ref.shape
    j = pl.program_id(1)

    @pl.when(pl.program_id(0) % tiles_per_seq == 0)
    def _():
        tail_g[j] = jnp.zeros(tail_g.shape[1:], F32)
        tail_u[j] = jnp.zeros(tail_u.shape[1:], F32)

    def tile(cols):
        a = hn_ref[...]
        cs = slice(0, cols)
        part = tm // ROW_PARTS

        def dots(w):
            return jnp.concatenate([jnp.dot(a[p * part:(p + 1) * part], w, preferred_element_type=F32)
                                    for p in range(ROW_PARTS)], axis=0)

        if cols >= 2 * MXU_TILE_V7X:
            ug, uu = dots(wg_ref[:, cs]), dots(wu_ref[:, cs])
        else:
            u2 = dots(jnp.concatenate([wg_ref[:, cs], wu_ref[:, cs]], axis=1))
            ug, uu = u2[:, :cols], u2[:, cols:]

        def conv(u, cw_ref, cb_ref, tail_ref):
            ext = jnp.concatenate([tail_ref[j, :, cs], u], axis=0)
            tail_ref[j, :, cs] = u[tm - TAIL:]
            cw = cw_ref[:, cs]
            y = cb_ref[:, cs] + cw[0:1] * ext[TAIL - 2:TAIL - 2 + tm]
            y = y + cw[1:2] * ext[TAIL - 1:TAIL - 1 + tm]
            return y + cw[2:3] * ext[TAIL:TAIL + tm]

        gate = conv(ug, cwg_ref, cbg_ref, tail_g)
        up = conv(uu, cwu_ref, cbu_ref, tail_u)
        o_ref[:, cs] = (jax.nn.silu(gate) * up).astype(o_ref.dtype)

    if last_cols == tn:
        tile(tn)
    else:
        last = pl.num_programs(1) - 1
        pl.when(j < last)(lambda: tile(tn))
        pl.when(j == last)(lambda: tile(last_cols))


def ffn_up(hn, w_gate, w_upper, cw_gate, cw_upper, cb_gate, cb_upper, *, seq, tm=TM, tn=TN_FFN):
    m, d = hn.shape
    ff = w_gate.shape[1]
    tm = min(tm, seq)
    assert seq % tm == 0 and ff % 128 == 0
    nj = pl.cdiv(ff, tn)
    col = lambda rows: pl.BlockSpec((rows, tn), lambda i, j: (0, j))
    return pl.pallas_call(
        functools.partial(_ffn_up_kernel, tiles_per_seq=seq // tm, last_cols=ff - (nj - 1) * tn),
        grid=(m // tm, nj),
        in_specs=[pl.BlockSpec((tm, d), lambda i, j: (i, 0)),
                  col(d), col(d), col(CONV_W), col(CONV_W), col(1), col(1)],
        out_specs=pl.BlockSpec((tm, tn), lambda i, j: (i, j)),
        out_shape=jax.ShapeDtypeStruct((m, ff), BF16),
        scratch_shapes=[pltpu.VMEM((nj, TAIL, tn), F32), pltpu.VMEM((nj, TAIL, tn), F32)],
        compiler_params=_cparams(("arbitrary", "arbitrary")),
        name="ffn_up",
    )(hn, w_gate, w_upper, cw_gate, cw_upper, cb_gate, cb_upper)


NEG = -1e30
N_BIAS_HEADS = A_HEADS + C_Q_HEADS


def _bucket_tiles():
    qi = jnp.arange(BLOCK)[:, None]
    kj = jnp.arange(2 * BLOCK)[None, :]
    rel = qi + BLOCK - kj
    max_exact = REL_BUCKETS // 2
    tiles = []
    for dilation, span in [(d, w // d) for w, d in DIL_GROUPS] + [(1, C_WINDOW - 1)]:
        dist = jnp.maximum(rel * dilation, 0)
        far = max_exact + (jnp.log(jnp.maximum(dist, 1).astype(F32) / max_exact)
                           / math.log(REL_MAX_DIST / max_exact) * (REL_BUCKETS - max_exact)).astype(jnp.int32)
        bucket = jnp.where(dist < max_exact, dist, jnp.minimum(far, REL_BUCKETS - 1))
        tiles.append(jnp.where((rel >= 0) & (rel <= span), bucket, -1))
    return jnp.stack(tiles).astype(jnp.int32)


def _bias_kernel(tab_ref, idx_ref, o_ref):
    h = pl.program_id(0)
    idx = idx_ref[...]
    acc = jnp.full(idx.shape, NEG, F32)
    for bkt in range(REL_BUCKETS):
        acc = jnp.where(idx == bkt, tab_ref[bkt, h], acc)
    col = lax.broadcasted_iota(jnp.int32, idx.shape, 1)
    o_ref[0] = jnp.where(col < BLOCK, NEG, acc)
    o_ref[1] = acc


def bias_tiles(rel_bias):
    groups = len(DIL_GROUPS)
    return pl.pallas_call(
        _bias_kernel,
        grid=(N_BIAS_HEADS,),
        in_specs=[pl.BlockSpec(memory_space=pltpu.SMEM),
                  pl.BlockSpec((None, BLOCK, 2 * BLOCK),
                               lambda h: (jnp.minimum(h // A_HEADS_PER_GROUP, groups), 0, 0))],
        out_specs=pl.BlockSpec((2, None, BLOCK, 2 * BLOCK), lambda h: (0, h, 0, 0)),
        out_shape=jax.ShapeDtypeStruct((2, N_BIAS_HEADS, BLOCK, 2 * BLOCK), F32),
        compiler_params=_cparams(("arbitrary",)),
        name="bias_tiles",
    )(rel_bias, _bucket_tiles())


ATT_SCALE = HEAD_DIM ** -0.5
HEADS_PER_STEP = 4
STEP_COLS = HEADS_PER_STEP * HEAD_DIM


def _scores(q, kp, kc, bias):
    k = jnp.concatenate([kp, kc], axis=0)
    s = lax.dot_general(q, k, (((1,), (1,)), ((), ())), preferred_element_type=F32)
    return s * ATT_SCALE + bias


def _attn_a_kernel(q_ref, kc_ref, kp_ref, vc_ref, vp_ref, bias_ref, o_ref, lse_ref, *, dilation, heads, qblocks):
    span = BLOCK * dilation
    first = jnp.minimum(pl.program_id(1), 1)
    for sb in range(qblocks):
        for r in range(dilation):
            def rows(blk):
                start = blk * span + r
                return pl.ds(start, BLOCK, stride=dilation) if dilation > 1 else pl.ds(start, BLOCK)

            for h in range(heads):
                sl = slice(h * HEAD_DIM, (h + 1) * HEAD_DIM)
                ld = lambda ref, blk: ref[rows(blk), sl].astype(BF16)
                prev = (lambda ref_p, ref_c: ld(ref_p, 0)) if sb == 0 else (lambda ref_p, ref_c: ld(ref_c, sb - 1))
                bias = bias_ref[first, h] if sb == 0 else bias_ref[1, h]
                s = _scores(ld(q_ref, sb), prev(kp_ref, kc_ref), ld(kc_ref, sb), bias)
                m = jnp.max(s, axis=-1, keepdims=True)
                p = jnp.exp(s - m)
                l = jnp.sum(p, axis=-1, keepdims=True)
                v = jnp.concatenate([prev(vp_ref, vc_ref), ld(vc_ref, sb)], axis=0)
                o = jnp.dot(p.astype(BF16), v, preferred_element_type=F32)
                o_ref[rows(sb), sl] = o / l
                lse_ref[rows(sb), sl] = jnp.broadcast_to(m + jnp.log(l), (BLOCK, HEAD_DIM))


def attn_dilated_group(pg, bias, gi, dilation, b, s):
    span = BLOCK * dilation
    qblocks = 4 if dilation < 16 else 1
    rows = span * qblocks
    heads = HEADS_PER_STEP if dilation == 1 else 1
    cols = heads * HEAD_DIM
    hs = HEADS_PER_STEP // heads
    pgv = pg.reshape(b, s, 3 * STEP_COLS)

    def cur(sec):
        return pl.BlockSpec((None, rows, cols), lambda bi, n, h: (bi, n, sec * hs + h))

    def prev(sec):
        return pl.BlockSpec((None, span, cols), lambda bi, n, h: (bi, jnp.maximum(n * qblocks - 1, 0), sec * hs + h))

    out_spec = pl.BlockSpec((None, rows, cols), lambda bi, n, h: (bi, n, h))
    out_sds = jax.ShapeDtypeStruct((b, s, STEP_COLS), F32)
    o, lse = pl.pallas_call(
        functools.partial(_attn_a_kernel, dilation=dilation, heads=heads, qblocks=qblocks),
        grid=(b, s // rows, hs),
        in_specs=[cur(0), cur(1), prev(1), cur(2), prev(2),
                  pl.BlockSpec((2, heads, BLOCK, 2 * BLOCK), lambda bi, n, h: (0, gi * hs + h, 0, 0))],
        out_specs=[out_spec, out_spec],
        out_shape=[out_sds, out_sds],
        compiler_params=_cparams(("parallel", "arbitrary", "arbitrary")),
        name=f"attn_dil{dilation}",
    )(pgv, pgv, pgv, pgv, pgv, bias)
    return o.reshape(b * s, STEP_COLS), lse.reshape(b * s, STEP_COLS)


def _lse_mix_kernel(o0_ref, o1_ref, o2_ref, l0_ref, l1_ref, l2_ref, o_ref):
    l0, l1, l2 = l0_ref[...], l1_ref[...], l2_ref[...]
    mx = jnp.maximum(jnp.maximum(l0, l1), l2)
    e0, e1, e2 = jnp.exp(l0 - mx), jnp.exp(l1 - mx), jnp.exp(l2 - mx)
    num = e0 * o0_ref[...] + e1 * o1_ref[...] + e2 * o2_ref[...]
    o_ref[...] = (num / (e0 + e1 + e2)).astype(o_ref.dtype)


def lse_mix(outs, lses, *, tr=4 * TR_ROWWISE):
    m, c = outs[0].shape
    tr = min(tr, m)
    spec = pl.BlockSpec((tr, c), lambda i: (i, 0))
    return pl.pallas_call(
        _lse_mix_kernel, grid=(m // tr,),
        in_specs=[spec] * 6, out_specs=spec,
        out_shape=jax.ShapeDtypeStruct((m, c), BF16),
        compiler_params=_cparams(("parallel",)), name="lse_mix",
    )(*outs, *lses)


SWA_QBLOCKS = 4


def _attn_c_kernel(sink_ref, q_ref, kc_ref, kp_ref, vc_ref, vp_ref, bias_ref, o_ref):
    kh = pl.program_id(2)
    first = jnp.minimum(pl.program_id(1), 1)
    for sb in range(SWA_QBLOCKS):
        rows = pl.ds(sb * BLOCK, BLOCK)
        before = pl.ds((sb - 1) * BLOCK, BLOCK)
        kprev, vprev = (kp_ref[...], vp_ref[...]) if sb == 0 else (kc_ref[before, :], vc_ref[before, :])
        v = jnp.concatenate([vprev, vc_ref[rows, :]], axis=0)
        for gq in range(HEADS_PER_STEP):
            sl = slice(gq * HEAD_DIM, (gq + 1) * HEAD_DIM)
            bias = bias_ref[first, gq] if sb == 0 else bias_ref[1, gq]
            s = _scores(q_ref[rows, sl], kprev, kc_ref[rows, :], bias)
            sink = sink_ref[kh * HEADS_PER_STEP + gq]
            m = jnp.maximum(jnp.max(s, axis=-1, keepdims=True), sink)
            w = jnp.exp(s - m)
            den = jnp.sum(w, axis=-1, keepdims=True) + jnp.exp(sink - m)
            o = jnp.dot(w.astype(BF16), v, preferred_element_type=F32)
            o_ref[rows, sl] = (o / den).astype(o_ref.dtype)


def attn_swa(pc, bias, sinks, b, s):
    rows = BLOCK * SWA_QBLOCKS
    pcv = pc.reshape(b, s, C_COLS)
    koff = C_Q_HEADS
    voff = C_Q_HEADS + C_KV_HEADS

    def cur(off):
        return pl.BlockSpec((None, rows, HEAD_DIM), lambda bi, n, kh: (bi, n, off + kh))

    def prev(off):
        return pl.BlockSpec((None, BLOCK, HEAD_DIM),
                            lambda bi, n, kh: (bi, jnp.maximum(n * SWA_QBLOCKS - 1, 0), off + kh))

    o = pl.pallas_call(
        _attn_c_kernel,
        grid=(b, s // rows, C_KV_HEADS),
        in_specs=[pl.BlockSpec(memory_space=pltpu.SMEM),
                  pl.BlockSpec((None, rows, STEP_COLS), lambda bi, n, kh: (bi, n, kh)),
                  cur(koff), prev(koff), cur(voff), prev(voff),
                  pl.BlockSpec((2, HEADS_PER_STEP, BLOCK, 2 * BLOCK),
                               lambda bi, n, kh: (0, A_HEADS // HEADS_PER_STEP + kh, 0, 0))],
        out_specs=pl.BlockSpec((None, rows, STEP_COLS), lambda bi, n, kh: (bi, n, kh)),
        out_shape=jax.ShapeDtypeStruct((b, s, C_OUT), BF16),
        compiler_params=_cparams(("parallel", "arbitrary", "arbitrary")),
        name="attn_swa",
    )(sinks, pcv, pcv, pcv, pcv, pcv, bias)
    return o.reshape(b * s, C_OUT)


B_QK = B_HEADS * B_DK


def _split3(x):
    hi = x.astype(BF16)
    r1 = x - hi.astype(F32)
    mid = r1.astype(BF16)
    lo = (r1 - mid.astype(F32)).astype(BF16)
    return hi, mid, lo


GLA_SUB = 4


def _gla_kernel(q_ref, k_ref, v_ref, r_ref, bg_ref, wg_ref, bgate_ref, gnorm_ref, o_ref, state_ref):
    @pl.when(pl.program_id(1) == 0)
    def _():
        state_ref[...] = jnp.zeros_like(state_ref)

    rows = B_CHUNK * GLA_SUB
    row = lax.broadcasted_iota(jnp.int32, (rows, rows), 0)
    coli = lax.broadcasted_iota(jnp.int32, (rows, rows), 1)
    causal = (row >= coli) & (row // B_CHUNK == coli // B_CHUNK)
    tril = jnp.where(causal, 1.0, 0.0).astype(BF16)

    z = jnp.dot(bg_ref[...].astype(BF16), wg_ref[...], preferred_element_type=F32) + bgate_ref[...]
    log_a = (jnp.minimum(z, 0.0) - jnp.log(1.0 + jnp.exp(-jnp.abs(z)))) / B_GATE_TAU
    cum = sum(jnp.dot(tril, part, preferred_element_type=F32) for part in _split3(log_a))
    lasts = [cum[(sc + 1) * B_CHUNK - 1:(sc + 1) * B_CHUNK, :] for sc in range(GLA_SUB)]
    last = jnp.concatenate([jnp.broadcast_to(l, (B_CHUNK, B_QK)) for l in lasts], axis=0)

    qf = q_ref[...].astype(F32) * (B_DK ** -0.5)
    kf = k_ref[...].astype(F32)
    q_dec = (qf * jnp.exp(cum)).astype(BF16)
    k_inv = (kf * jnp.exp(-cum)).astype(BF16)
    k_out = (kf * jnp.exp(last - cum)).astype(BF16)

    for h in range(B_HEADS):
        ks = slice(h * B_DK, (h + 1) * B_DK)
        vs = slice(h * B_DV, (h + 1) * B_DV)
        qd, ki, ko, vh = q_dec[:, ks], k_inv[:, ks], k_out[:, ks], v_ref[:, vs]
        att = lax.dot_general(qd, ki, (((1,), (1,)), ((), ())), preferred_element_type=F32)
        att = jnp.where(causal, att, 0.0).astype(BF16)
        o_intra = jnp.dot(att, vh, preferred_element_type=F32)
        o_inter = []
        st = state_ref[h]
        for sc in range(GLA_SUB):
            cr = slice(sc * B_CHUNK, (sc + 1) * B_CHUNK)
            o_inter.append(lax.dot_general(qd[cr], st.astype(BF16), (((1,), (1,)), ((), ())),
                                           preferred_element_type=F32))
            upd = lax.dot_general(vh[cr], ko[cr], (((0,), (0,)), ((), ())), preferred_element_type=F32)
            st = st * jnp.exp(lasts[sc][:, ks]) + upd
        state_ref[h] = st
        o = o_intra + jnp.concatenate(o_inter, axis=0)
        on = _rms(o, gnorm_ref[:, vs])
        o_ref[:, vs] = (on * jax.nn.silu(r_ref[:, vs].astype(F32))).astype(o_ref.dtype)


def gla(pb, bg, w_gate, b_gate, gnorm, b, s):
    rows = B_CHUNK * GLA_SUB
    pbv = pb.reshape(b, s, B_COLS)
    bgv = bg.reshape(b, s, B_GATE_RANK)
    wg = w_gate.astype(BF16)
    full = lambda shape: pl.BlockSpec(shape, lambda bi, c: (0, 0))
    o = pl.pallas_call(
        _gla_kernel,
        grid=(b, s // rows),
        in_specs=[pl.BlockSpec((None, rows, B_QK), lambda bi, c: (bi, c, 0)),
                  pl.BlockSpec((None, rows, B_QK), lambda bi, c: (bi, c, 1)),
                  pl.BlockSpec((None, rows, B_OUT), lambda bi, c: (bi, c, 1)),
                  pl.BlockSpec((None, rows, B_OUT), lambda bi, c: (bi, c, 2)),
                  pl.BlockSpec((None, rows, B_GATE_RANK), lambda bi, c: (bi, c, 0)),
                  full((B_GATE_RANK, B_QK)), full((1, B_QK)), full((1, B_OUT))],
        out_specs=pl.BlockSpec((None, rows, B_OUT), lambda bi, c: (bi, c, 0)),
        out_shape=jax.ShapeDtypeStruct((b, s, B_OUT), BF16),
        scratch_shapes=[pltpu.VMEM((B_HEADS, B_DV, B_DK), F32)],
        compiler_params=_cparams(("parallel", "arbitrary")),
        name="gla",
    )(pbv, pbv, pbv, pbv, bgv, wg, b_gate.reshape(1, B_QK), jnp.tile(gnorm, B_HEADS).reshape(1, B_OUT))
    return o.reshape(b * s, B_OUT)


def token_mixers(pgs, pb, pc, bg, bias, w_gla_gate, b_gla_gate, gla_norm, attn_sinks, b, s):
    outs, lses = [], []
    for gi, (_, dilation) in enumerate(DIL_GROUPS):
        o, lse = attn_dilated_group(pgs[gi], bias, gi, dilation, b, s)
        outs.append(o)
        lses.append(lse)
    o_a = lse_mix(outs, lses)
    o_b = gla(pb, bg, w_gla_gate, b_gla_gate, gla_norm, b, s)
    o_c = attn_swa(pc, bias, attn_sinks, b, s)
    return o_a, o_b, o_c


def _cast_kernel(w_ref, o_ref):
    o_ref[...] = w_ref[...].astype(o_ref.dtype)


def cast_layer(w, l, *, tr=TR_ROWWISE):
    _, rows, cols = w.shape
    return pl.pallas_call(
        _cast_kernel, grid=(rows // tr,),
        in_specs=[pl.BlockSpec((None, tr, cols), lambda i: (l, i, 0))],
        out_specs=pl.BlockSpec((tr, cols), lambda i: (i, 0)),
        out_shape=jax.ShapeDtypeStruct((rows, cols), BF16),
        compiler_params=_cparams(("parallel",)), name="cast_w",
    )(w)


def _cast_halves_kernel(w_ref, lo_ref, hi_ref):
    half = lo_ref.shape[1]
    lo_ref[...] = w_ref[:, :half].astype(lo_ref.dtype)
    hi_ref[...] = w_ref[:, half:].astype(hi_ref.dtype)


def cast_halves(w, l, *, tr=64):
    _, rows, cols = w.shape
    half = cols // 2
    out = pl.BlockSpec((tr, half), lambda i: (i, 0))
    sds = jax.ShapeDtypeStruct((rows, half), BF16)
    return pl.pallas_call(
        _cast_halves_kernel, grid=(rows // tr,),
        in_specs=[pl.BlockSpec((None, tr, cols), lambda i: (l, i, 0))],
        out_specs=[out, out], out_shape=[sds, sds],
        compiler_params=_cparams(("parallel",)), name="cast_halves",
    )(w)


A_SEC = A_HEADS * HEAD_DIM


W_IN_STRIP = 128


def _cast_w_in_kernel(w_ref, a0_ref, a1_ref, a2_ref, b_ref, bg_ref, c_ref, g_ref):
    cvt = lambda lo, n: w_ref[lo:lo + n, :].astype(BF16)
    for gi, a_ref in enumerate((a0_ref, a1_ref, a2_ref)):
        for sec in range(3):
            a_ref[sec * STEP_COLS:(sec + 1) * STEP_COLS, :] = cvt(sec * A_SEC + gi * STEP_COLS, STEP_COLS)
    b_ref[...] = cvt(OFF_B, B_COLS)
    bg_ref[...] = cvt(OFF_BG, B_GATE_RANK)
    c_ref[...] = cvt(OFF_C, C_COLS)
    g_ref[...] = cvt(OFF_G, G_COLS)


def cast_w_in(w_in_t, l):
    _, n_in, d = w_in_t.shape
    heights = (3 * STEP_COLS,) * len(DIL_GROUPS) + (B_COLS, B_GATE_RANK, C_COLS, G_COLS)
    return pl.pallas_call(
        _cast_w_in_kernel, grid=(d // W_IN_STRIP,),
        in_specs=[pl.BlockSpec((None, n_in, W_IN_STRIP), lambda j: (l, 0, j))],
        out_specs=[pl.BlockSpec((n, W_IN_STRIP), lambda j: (0, j)) for n in heights],
        out_shape=[jax.ShapeDtypeStruct((n, d), BF16) for n in heights],
        compiler_params=_cparams(("parallel",)), name="cast_w_in",
    )(w_in_t)


def kernel(x, rel_bias, w_in, w_gla_gate, b_gla_gate, gla_norm, attn_sinks, w_br_a, w_br_b, w_br_c, w_out,
           g_pre_mix, g_post_mix, g_pre_ffn, g_post_ffn, w_up, conv_w, conv_b, w_down):
    b, s, d = x.shape
    depth = w_in.shape[0]
    xf = x.reshape(b * s, d)
    xn = rms_cast(xf, g_pre_mix[0])
    bias = bias_tiles(rel_bias)
    w_in_t = jnp.swapaxes(w_in, 1, 2)
    for l in range(depth):
        *w_a, w_b, w_bg, w_c, w_g = cast_w_in(w_in_t, l)
        proj = functools.partial(matmul, xn, trans_b=True)
        pgs = [proj(w_a[gi], tn=TN_3, out_dtype=BF16 if dil == 1 else F32, name=f"proj_a{gi}")
               for gi, (_, dil) in enumerate(DIL_GROUPS)]
        pb = proj(w_b, name="proj_b")
        pc = proj(w_c, tn=TN_3, name="proj_c")
        gates = proj(w_g, act="sigmoid", name="proj_g")
        bg = proj(w_bg, out_dtype=F32, name="proj_bg")

        o_a, o_b, o_c = token_mixers(pgs, pb, pc, bg, bias, w_gla_gate[l], b_gla_gate[l], gla_norm[l],
                                     attn_sinks[l], b, s)

        merged = merge(o_a, o_b, o_c, gates, cast_layer(w_br_a, l), cast_layer(w_br_b, l), cast_layer(w_br_c, l))
        y = matmul(merged, cast_layer(w_out, l), name="proj_out")
        xf, hn = norm_residual(y, xf, g_post_mix[l], g_pre_ffn[l])

        w_gate, w_upper = cast_halves(w_up, l)
        h = ffn_up(hn, w_gate, w_upper, conv_w[l][:, :D_FF], conv_w[l][:, D_FF:],
                   conv_b[l][:D_FF].reshape(1, D_FF), conv_b[l][D_FF:].reshape(1, D_FF), seq=s)
        f = matmul(h, cast_layer(w_down, l), w_outer=True, tm=T_DOWN, tn=T_DOWN, name="ffn_down")
        xf, xn = norm_residual(f, xf, g_post_ffn[l], g_pre_mix[l + 1] if l + 1 < depth else None)
    return xf.reshape(b, s, d)
```

```python
import functools
import math

import jax
import jax.numpy as jnp
from jax import lax
from jax.experimental import pallas as pl
from jax.experimental.pallas import tpu as pltpu

F32 = jnp.float32
BF16 = jnp.bfloat16

D_MODEL = 4096
HEAD_DIM = 128
BLOCK = 128
NORM_EPS = 1e-6
DIL_GROUPS = ((128, 1), (512, 4), (2048, 16))
A_HEADS_PER_GROUP = 4
A_HEADS = 12
B_HEADS = 8
B_DK = 64
B_DV = 128
B_GATE_RANK = 16
B_GATE_TAU = 16.0
B_CHUNK = 64
B_OUT = 1024
C_Q_HEADS = 12
C_KV_HEADS = 3
C_WINDOW = 128
C_OUT = 1536
REL_BUCKETS = 32
REL_MAX_DIST = 2048
D_FF = 11008
CONV_W = 3

A_COLS = 3 * A_HEADS * HEAD_DIM
B_COLS = 2 * B_HEADS * B_DK + 2 * B_HEADS * B_DV
C_COLS = (C_Q_HEADS + 2 * C_KV_HEADS) * HEAD_DIM
G_COLS = 3 * D_MODEL
OFF_B = A_COLS
OFF_BG = OFF_B + B_COLS
OFF_C = OFF_BG + B_GATE_RANK
OFF_G = OFF_C + C_COLS


MXU_TILE_V7X = 256
VMEM_LIMIT = 56 * 1024 * 1024

TM = 1024
TN = 4 * MXU_TILE_V7X
TN_3 = 3 * MXU_TILE_V7X
TN_FFN = 2 * MXU_TILE_V7X
T_DOWN = 2 * MXU_TILE_V7X
TR_ROWWISE = 256


def _cparams(sem):
    return pltpu.CompilerParams(dimension_semantics=sem, vmem_limit_bytes=VMEM_LIMIT)


ROW_PARTS = 4


def _row_parts(tm):
    n = ROW_PARTS if tm % (ROW_PARTS * 128) == 0 else 1
    return [pl.ds(p * (tm // n), tm // n) for p in range(n)]


def _rms(x, g):
    ms = jnp.mean(x * x, axis=-1, keepdims=True)
    return x * lax.rsqrt(ms + NORM_EPS) * g


def _rms_cast_kernel(x_ref, g_ref, o_ref):
    o_ref[...] = _rms(x_ref[...], g_ref[...]).astype(o_ref.dtype)


def rms_cast(x, g, *, tr=TR_ROWWISE):
    m, d = x.shape
    return pl.pallas_call(
        _rms_cast_kernel,
        grid=(m // tr,),
        in_specs=[pl.BlockSpec((tr, d), lambda i: (i, 0)),
                  pl.BlockSpec((1, d), lambda i: (0, 0))],
        out_specs=pl.BlockSpec((tr, d), lambda i: (i, 0)),
        out_shape=jax.ShapeDtypeStruct((m, d), BF16),
        compiler_params=_cparams(("parallel",)),
        name="rms_cast",
    )(x, g.reshape(1, d))


def _norm_res_kernel(y_ref, x_ref, g_ref, g2_ref, xo_ref, hn_ref):
    xn = x_ref[...] + _rms(y_ref[...].astype(F32), g_ref[...])
    xo_ref[...] = xn
    hn_ref[...] = _rms(xn, g2_ref[...]).astype(hn_ref.dtype)


def _norm_res_last_kernel(y_ref, x_ref, g_ref, xo_ref):
    xo_ref[...] = x_ref[...] + _rms(y_ref[...].astype(F32), g_ref[...])


def norm_residual(y, x, g, g_next=None, *, tr=TR_ROWWISE):
    m, d = x.shape
    row = pl.BlockSpec((tr, d), lambda i: (i, 0))
    vec = pl.BlockSpec((1, d), lambda i: (0, 0))
    if g_next is None:
        return pl.pallas_call(
            _norm_res_last_kernel, grid=(m // tr,),
            in_specs=[row, row, vec], out_specs=row,
            out_shape=jax.ShapeDtypeStruct((m, d), F32),
            compiler_params=_cparams(("parallel",)), name="norm_res_last",
        )(y, x, g.reshape(1, d)), None
    return pl.pallas_call(
        _norm_res_kernel, grid=(m // tr,),
        in_specs=[row, row, vec, vec], out_specs=[row, row],
        out_shape=[jax.ShapeDtypeStruct((m, d), F32), jax.ShapeDtypeStruct((m, d), BF16)],
        compiler_params=_cparams(("parallel",)), name="norm_res",
    )(y, x, g.reshape(1, d), g_next.reshape(1, d))


def _mm_kernel(a_ref, w_ref, o_ref, *, act, trans_b):
    contract = (((1,), (1 if trans_b else 0,)), ((), ()))
    w = w_ref[...]
    for rows in _row_parts(a_ref.shape[0]):
        acc = lax.dot_general(a_ref[rows, :], w, contract, preferred_element_type=F32)
        if act == "sigmoid":
            acc = jax.nn.sigmoid(acc)
        o_ref[rows, :] = acc.astype(o_ref.dtype)


def matmul(a, w, *, trans_b=False, w_outer=False, tm=TM, tn=TN, out_dtype=BF16, act=None, name="mm"):
    m, k = a.shape
    n = w.shape[0 if trans_b else 1]
    tm = min(tm, m)
    tn = min(tn, n)
    assert m % tm == 0 and n % tn == 0
    ij = (lambda jo, ii: (ii, jo)) if w_outer else (lambda io, ji: (io, ji))
    a_map = lambda *g: (ij(*g)[0], 0)
    w_map = (lambda *g: (ij(*g)[1], 0)) if trans_b else (lambda *g: (0, ij(*g)[1]))
    return pl.pallas_call(
        functools.partial(_mm_kernel, act=act, trans_b=trans_b),
        grid=(n // tn, m // tm) if w_outer else (m // tm, n // tn),
        in_specs=[pl.BlockSpec((tm, k), a_map), pl.BlockSpec((tn, k) if trans_b else (k, tn), w_map)],
        out_specs=pl.BlockSpec((tm, tn), lambda *g: ij(*g)),
        out_shape=jax.ShapeDtypeStruct((m, n), out_dtype),
        compiler_params=_cparams(("parallel", "arbitrary")),
        name=name,
    )(a, w)


def _merge_kernel(oa_ref, ob_ref, oc_ref, ga_ref, gb_ref, gc_ref, wa_ref, wb_ref, wc_ref, o_ref):
    wa, wb, wc = wa_ref[...], wb_ref[...], wc_ref[...]
    for rows in _row_parts(o_ref.shape[0]):
        gated = lambda g_ref, o_in_ref, w: g_ref[rows, :].astype(F32) * jnp.dot(o_in_ref[rows, :], w,
                                                                                preferred_element_type=F32)
        acc = gated(ga_ref, oa_ref, wa)
        acc = acc + gated(gb_ref, ob_ref, wb)
        acc = acc + gated(gc_ref, oc_ref, wc)
        o_ref[rows, :] = acc.astype(o_ref.dtype)


def merge(o_a, o_b, o_c, gates, wa, wb, wc, *, tm=TM, tn=TN):
    m = o_a.shape[0]
    d = wa.shape[1]
    tm = min(tm, m)
    nj = d // tn
    act = lambda kdim: pl.BlockSpec((tm, kdim), lambda i, j: (i, 0))
    gate = lambda s: pl.BlockSpec((tm, tn), lambda i, j: (i, s * nj + j))
    wgt = lambda kdim: pl.BlockSpec((kdim, tn), lambda i, j: (0, j))
    return pl.pallas_call(
        _merge_kernel,
        grid=(m // tm, nj),
        in_specs=[act(o_a.shape[1]), act(o_b.shape[1]), act(o_c.shape[1]),
                  gate(0), gate(1), gate(2),
                  wgt(wa.shape[0]), wgt(wb.shape[0]), wgt(wc.shape[0])],
        out_specs=pl.BlockSpec((tm, tn), lambda i, j: (i, j)),
        out_shape=jax.ShapeDtypeStruct((m, d), BF16),
        compiler_params=_cparams(("parallel", "arbitrary")),
        name="merge",
    )(o_a, o_b, o_c, gates, gates, gates, wa, wb, wc)


TAIL = 8


def _ffn_up_kernel(hn_ref, wg_ref, wu_ref, cwg_ref, cwu_ref, cbg_ref, cbu_ref, o_ref, tail_g, tail_u, *,
                   tiles_per_seq, last_cols):
    tm, tn = o_ref.shape
    j = pl.program_id(1)

    @pl.when(pl.program_id(0) % tiles_per_seq == 0)
    def _():
        tail_g[j] = jnp.zeros(tail_g.shape[1:], F32)
        tail_u[j] = jnp.zeros(tail_u.shape[1:], F32)

    def tile(cols):
        a = hn_ref[...]
        cs = slice(0, cols)
        parts = 2 * ROW_PARTS
        part = tm // parts

        def dots(w):
            return jnp.concatenate([jnp.dot(a[p * part:(p + 1) * part], w, preferred_element_type=F32)
                                    for p in range(parts)], axis=0)

        if cols >= 2 * MXU_TILE_V7X:
            ug, uu = dots(wg_ref[:, cs]), dots(wu_ref[:, cs])
        else:
            u2 = dots(jnp.concatenate([wg_ref[:, cs], wu_ref[:, cs]], axis=1))
            ug, uu = u2[:, :cols], u2[:, cols:]

        def conv(u, cw_ref, cb_ref, tail_ref):
            ext = jnp.concatenate([tail_ref[j, :, cs], u], axis=0)
            tail_ref[j, :, cs] = u[tm - TAIL:]
            cw = cw_ref[:, cs]
            y = cb_ref[:, cs] + cw[0:1] * ext[TAIL - 2:TAIL - 2 + tm]
            y = y + cw[1:2] * ext[TAIL - 1:TAIL - 1 + tm]
            return y + cw[2:3] * ext[TAIL:TAIL + tm]

        gate = conv(ug, cwg_ref, cbg_ref, tail_g)
        up = conv(uu, cwu_ref, cbu_ref, tail_u)
        o_ref[:, cs] = (jax.nn.silu(gate) * up).astype(o_ref.dtype)

    if last_cols == tn:
        tile(tn)
    else:
        last = pl.num_programs(1) - 1
        pl.when(j < last)(lambda: tile(tn))
        pl.when(j == last)(lambda: tile(last_cols))


def ffn_up(hn, w_gate, w_upper, cw_gate, cw_upper, cb_gate, cb_upper, *, seq, tm=TM, tn=TN_FFN):
    m, d = hn.shape
    ff = w_gate.shape[1]
    tm = min(tm, seq)
    assert seq % tm == 0 and ff % 128 == 0
    nj = pl.cdiv(ff, tn)
    col = lambda rows: pl.BlockSpec((rows, tn), lambda i, j: (0, j))
    return pl.pallas_call(
        functools.partial(_ffn_up_kernel, tiles_per_seq=seq // tm, last_cols=ff - (nj - 1) * tn),
        grid=(m // tm, nj),
        in_specs=[pl.BlockSpec((tm, d), lambda i, j: (i, 0)),
                  col(d), col(d), col(CONV_W), col(CONV_W), col(1), col(1)],
        out_specs=pl.BlockSpec((tm, tn), lambda i, j: (i, j)),
        out_shape=jax.ShapeDtypeStruct((m, ff), BF16),
        scratch_shapes=[pltpu.VMEM((nj, TAIL, tn), F32), pltpu.VMEM((nj, TAIL, tn), F32)],
        compiler_params=_cparams(("arbitrary", "arbitrary")),
        name="ffn_up",
    )(hn, w_gate, w_upper, cw_gate, cw_upper, cb_gate, cb_upper)


NEG = -1e30
N_BIAS_HEADS = A_HEADS + C_Q_HEADS


def _bucket_tiles():
    qi = jnp.arange(BLOCK)[:, None]
    kj = jnp.arange(2 * BLOCK)[None, :]
    rel = qi + BLOCK - kj
    max_exact = REL_BUCKETS // 2
    tiles = []
    for dilation, span in [(d, w // d) for w, d in DIL_GROUPS] + [(1, C_WINDOW - 1)]:
        dist = jnp.maximum(rel * dilation, 0)
        far = max_exact + (jnp.log(jnp.maximum(dist, 1).astype(F32) / max_exact)
                           / math.log(REL_MAX_DIST / max_exact) * (REL_BUCKETS - max_exact)).astype(jnp.int32)
        bucket = jnp.where(dist < max_exact, dist, jnp.minimum(far, REL_BUCKETS - 1))
        tiles.append(jnp.where((rel >= 0) & (rel <= span), bucket, -1))
    return jnp.stack(tiles).astype(jnp.int32)


def _bias_kernel(tab_ref, idx_ref, o_ref):
    h = pl.program_id(0)
    idx = idx_ref[...]
    acc = jnp.full(idx.shape, NEG, F32)
    for bkt in range(REL_BUCKETS):
        acc = jnp.where(idx == bkt, tab_ref[bkt, h], acc)
    col = lax.broadcasted_iota(jnp.int32, idx.shape, 1)
    o_ref[0] = jnp.where(col < BLOCK, NEG, acc)
    o_ref[1] = acc


def bias_tiles(rel_bias):
    groups = len(DIL_GROUPS)
    return pl.pallas_call(
        _bias_kernel,
        grid=(N_BIAS_HEADS,),
        in_specs=[pl.BlockSpec(memory_space=pltpu.SMEM),
                  pl.BlockSpec((None, BLOCK, 2 * BLOCK),
                               lambda h: (jnp.minimum(h // A_HEADS_PER_GROUP, groups), 0, 0))],
        out_specs=pl.BlockSpec((2, None, BLOCK, 2 * BLOCK), lambda h: (0, h, 0, 0)),
        out_shape=jax.ShapeDtypeStruct((2, N_BIAS_HEADS, BLOCK, 2 * BLOCK), F32),
        compiler_params=_cparams(("arbitrary",)),
        name="bias_tiles",
    )(rel_bias, _bucket_tiles())


ATT_SCALE = HEAD_DIM ** -0.5
HEADS_PER_STEP = 4
STEP_COLS = HEADS_PER_STEP * HEAD_DIM


def _scores(q, kp, kc, bias):
    k = jnp.concatenate([kp, kc], axis=0)
    s = lax.dot_general(q, k, (((1,), (1,)), ((), ())), preferred_element_type=F32)
    return s * ATT_SCALE + bias


def _attn_a_kernel(q_ref, kc_ref, kp_ref, vc_ref, vp_ref, bias_ref, o_ref, lse_ref, *, dilation, heads, qblocks):
    span = BLOCK * dilation
    first = jnp.minimum(pl.program_id(1), 1)
    for sb in range(qblocks):
        for r in range(dilation):
            def rows(blk):
                start = blk * span + r
                return pl.ds(start, BLOCK, stride=dilation) if dilation > 1 else pl.ds(start, BLOCK)

            for h in range(heads):
                sl = slice(h * HEAD_DIM, (h + 1) * HEAD_DIM)
                ld = lambda ref, blk: ref[rows(blk), sl].astype(BF16)
                prev = (lambda ref_p, ref_c: ld(ref_p, 0)) if sb == 0 else (lambda ref_p, ref_c: ld(ref_c, sb - 1))
                bias = bias_ref[first, h] if sb == 0 else bias_ref[1, h]
                s = _scores(ld(q_ref, sb), prev(kp_ref, kc_ref), ld(kc_ref, sb), bias)
                m = jnp.max(s, axis=-1, keepdims=True)
                p = jnp.exp(s - m)
                l = jnp.sum(p, axis=-1, keepdims=True)
                v = jnp.concatenate([prev(vp_ref, vc_ref), ld(vc_ref, sb)], axis=0)
                o = jnp.dot(p.astype(BF16), v, preferred_element_type=F32)
                o_ref[rows(sb), sl] = o / l
                lse_ref[rows(sb), sl] = jnp.broadcast_to(m + jnp.log(l), (BLOCK, HEAD_DIM))


def attn_dilated_group(pg, bias, gi, dilation, b, s):
    span = BLOCK * dilation
    qblocks = 4 if dilation < 16 else 1
    rows = span * qblocks
    heads = HEADS_PER_STEP if dilation == 1 else 1
    cols = heads * HEAD_DIM
    hs = HEADS_PER_STEP // heads
    pgv = pg.reshape(b, s, 3 * STEP_COLS)

    def cur(sec):
        return pl.BlockSpec((None, rows, cols), lambda bi, n, h: (bi, n, sec * hs + h))

    def prev(sec):
        return pl.BlockSpec((None, span, cols), lambda bi, n, h: (bi, jnp.maximum(n * qblocks - 1, 0), sec * hs + h))

    out_spec = pl.BlockSpec((None, rows, cols), lambda bi, n, h: (bi, n, h))
    out_sds = jax.ShapeDtypeStruct((b, s, STEP_COLS), F32)
    o, lse = pl.pallas_call(
        functools.partial(_attn_a_kernel, dilation=dilation, heads=heads, qblocks=qblocks),
        grid=(b, s // rows, hs),
        in_specs=[cur(0), cur(1), prev(1), cur(2), prev(2),
                  pl.BlockSpec((2, heads, BLOCK, 2 * BLOCK), lambda bi, n, h: (0, gi * hs + h, 0, 0))],
        out_specs=[out_spec, out_spec],
        out_shape=[out_sds, out_sds],
        compiler_params=_cparams(("parallel", "arbitrary", "arbitrary")),
        name=f"attn_dil{dilation}",
    )(pgv, pgv, pgv, pgv, pgv, bias)
    return o.reshape(b * s, STEP_COLS), lse.reshape(b * s, STEP_COLS)


def _lse_mix_kernel(o0_ref, o1_ref, o2_ref, l0_ref, l1_ref, l2_ref, o_ref):
    l0, l1, l2 = l0_ref[...], l1_ref[...], l2_ref[...]
    mx = jnp.maximum(jnp.maximum(l0, l1), l2)
    e0, e1, e2 = jnp.exp(l0 - mx), jnp.exp(l1 - mx), jnp.exp(l2 - mx)
    num = e0 * o0_ref[...] + e1 * o1_ref[...] + e2 * o2_ref[...]
    o_ref[...] = (num / (e0 + e1 + e2)).astype(o_ref.dtype)


def lse_mix(outs, lses, *, tr=4 * TR_ROWWISE):
    m, c = outs[0].shape
    tr = min(tr, m)
    spec = pl.BlockSpec((tr, c), lambda i: (i, 0))
    return pl.pallas_call(
        _lse_mix_kernel, grid=(m // tr,),
        in_specs=[spec] * 6, out_specs=spec,
        out_shape=jax.ShapeDtypeStruct((m, c), BF16),
        compiler_params=_cparams(("parallel",)), name="lse_mix",
    )(*outs, *lses)


SWA_QBLOCKS = 4


def _attn_c_kernel(sink_ref, q_ref, kc_ref, kp_ref, vc_ref, vp_ref, bias_ref, o_ref):
    kh = pl.program_id(2)
    first = jnp.minimum(pl.program_id(1), 1)
    for sb in range(SWA_QBLOCKS):
        rows = pl.ds(sb * BLOCK, BLOCK)
        before = pl.ds((sb - 1) * BLOCK, BLOCK)
        kprev, vprev = (kp_ref[...], vp_ref[...]) if sb == 0 else (kc_ref[before, :], vc_ref[before, :])
        v = jnp.concatenate([vprev, vc_ref[rows, :]], axis=0)
        for gq in range(HEADS_PER_STEP):
            sl = slice(gq * HEAD_DIM, (gq + 1) * HEAD_DIM)
            bias = bias_ref[first, gq] if sb == 0 else bias_ref[1, gq]
            s = _scores(q_ref[rows, sl], kprev, kc_ref[rows, :], bias)
            sink = sink_ref[kh * HEADS_PER_STEP + gq]
            m = jnp.maximum(jnp.max(s, axis=-1, keepdims=True), sink)
            w = jnp.exp(s - m)
            den = jnp.sum(w, axis=-1, keepdims=True) + jnp.exp(sink - m)
            o = jnp.dot(w.astype(BF16), v, preferred_element_type=F32)
            o_ref[rows, sl] = (o / den).astype(o_ref.dtype)


def attn_swa(pc, bias, sinks, b, s):
    rows = BLOCK * SWA_QBLOCKS
    pcv = pc.reshape(b, s, C_COLS)
    koff = C_Q_HEADS
    voff = C_Q_HEADS + C_KV_HEADS

    def cur(off):
        return pl.BlockSpec((None, rows, HEAD_DIM), lambda bi, n, kh: (bi, n, off + kh))

    def prev(off):
        return pl.BlockSpec((None, BLOCK, HEAD_DIM),
                            lambda bi, n, kh: (bi, jnp.maximum(n * SWA_QBLOCKS - 1, 0), off + kh))

    o = pl.pallas_call(
        _attn_c_kernel,
        grid=(b, s // rows, C_KV_HEADS),
        in_specs=[pl.BlockSpec(memory_space=pltpu.SMEM),
                  pl.BlockSpec((None, rows, STEP_COLS), lambda bi, n, kh: (bi, n, kh)),
                  cur(koff), prev(koff), cur(voff), prev(voff),
                  pl.BlockSpec((2, HEADS_PER_STEP, BLOCK, 2 * BLOCK),
                               lambda bi, n, kh: (0, A_HEADS // HEADS_PER_STEP + kh, 0, 0))],
        out_specs=pl.BlockSpec((None, rows, STEP_COLS), lambda bi, n, kh: (bi, n, kh)),
        out_shape=jax.ShapeDtypeStruct((b, s, C_OUT), BF16),
        compiler_params=_cparams(("parallel", "arbitrary", "arbitrary")),
        name="attn_swa",
    )(sinks, pcv, pcv, pcv, pcv, pcv, bias)
    return o.reshape(b * s, C_OUT)


B_QK = B_HEADS * B_DK


def _split3(x):
    hi = x.astype(BF16)
    r1 = x - hi.astype(F32)
    mid = r1.astype(BF16)
    lo = (r1 - mid.astype(F32)).astype(BF16)
    return hi, mid, lo


GLA_SUB = 4


def _gla_kernel(q_ref, k_ref, v_ref, r_ref, bg_ref, wg_ref, bgate_ref, gnorm_ref, o_ref, state_ref):
    @pl.when(pl.program_id(1) == 0)
    def _():
        state_ref[...] = jnp.zeros_like(state_ref)

    rows = B_CHUNK * GLA_SUB
    row = lax.broadcasted_iota(jnp.int32, (rows, rows), 0)
    coli = lax.broadcasted_iota(jnp.int32, (rows, rows), 1)
    causal = (row >= coli) & (row // B_CHUNK == coli // B_CHUNK)
    tril = jnp.where(causal, 1.0, 0.0).astype(BF16)

    z = jnp.dot(bg_ref[...].astype(BF16), wg_ref[...], preferred_element_type=F32) + bgate_ref[...]
    log_a = (jnp.minimum(z, 0.0) - jnp.log(1.0 + jnp.exp(-jnp.abs(z)))) / B_GATE_TAU
    cum = sum(jnp.dot(tril, part, preferred_element_type=F32) for part in _split3(log_a))
    lasts = [cum[(sc + 1) * B_CHUNK - 1:(sc + 1) * B_CHUNK, :] for sc in range(GLA_SUB)]
    last = jnp.concatenate([jnp.broadcast_to(l, (B_CHUNK, B_QK)) for l in lasts], axis=0)

    qf = q_ref[...].astype(F32) * (B_DK ** -0.5)
    kf = k_ref[...].astype(F32)
    q_dec = (qf * jnp.exp(cum)).astype(BF16)
    k_inv = (kf * jnp.exp(-cum)).astype(BF16)
    k_out = (kf * jnp.exp(last - cum)).astype(BF16)

    for h in range(B_HEADS):
        ks = slice(h * B_DK, (h + 1) * B_DK)
        vs = slice(h * B_DV, (h + 1) * B_DV)
        qd, ki, ko, vh = q_dec[:, ks], k_inv[:, ks], k_out[:, ks], v_ref[:, vs]
        att = lax.dot_general(qd, ki, (((1,), (1,)), ((), ())), preferred_element_type=F32)
        att = jnp.where(causal, att, 0.0).astype(BF16)
        o_intra = jnp.dot(att, vh, preferred_element_type=F32)
        o_inter = []
        st = state_ref[h]
        for sc in range(GLA_SUB):
            cr = slice(sc * B_CHUNK, (sc + 1) * B_CHUNK)
            o_inter.append(lax.dot_general(qd[cr], st.astype(BF16), (((1,), (1,)), ((), ())),
                                           preferred_element_type=F32))
            upd = lax.dot_general(vh[cr], ko[cr], (((0,), (0,)), ((), ())), preferred_element_type=F32)
            st = st * jnp.exp(lasts[sc][:, ks]) + upd
        state_ref[h] = st
        o = o_intra + jnp.concatenate(o_inter, axis=0)
        on = _rms(o, gnorm_ref[:, vs])
        o_ref[:, vs] = (on * jax.nn.silu(r_ref[:, vs].astype(F32))).astype(o_ref.dtype)


def gla(pb, bg, w_gate, b_gate, gnorm, b, s):
    rows = B_CHUNK * GLA_SUB
    pbv = pb.reshape(b, s, B_COLS)
    bgv = bg.reshape(b, s, B_GATE_RANK)
    wg = w_gate.astype(BF16)
    full = lambda shape: pl.BlockSpec(shape, lambda bi, c: (0, 0))
    o = pl.pallas_call(
        _gla_kernel,
        grid=(b, s // rows),
        in_specs=[pl.BlockSpec((None, rows, B_QK), lambda bi, c: (bi, c, 0)),
                  pl.BlockSpec((None, rows, B_QK), lambda bi, c: (bi, c, 1)),
                  pl.BlockSpec((None, rows, B_OUT), lambda bi, c: (bi, c, 1)),
                  pl.BlockSpec((None, rows, B_OUT), lambda bi, c: (bi, c, 2)),
                  pl.BlockSpec((None, rows, B_GATE_RANK), lambda bi, c: (bi, c, 0)),
                  full((B_GATE_RANK, B_QK)), full((1, B_QK)), full((1, B_OUT))],
        out_specs=pl.BlockSpec((None, rows, B_OUT), lambda bi, c: (bi, c, 0)),
        out_shape=jax.ShapeDtypeStruct((b, s, B_OUT), BF16),
        scratch_shapes=[pltpu.VMEM((B_HEADS, B_DV, B_DK), F32)],
        compiler_params=_cparams(("parallel", "arbitrary")),
        name="gla",
    )(pbv, pbv, pbv, pbv, bgv, wg, b_gate.reshape(1, B_QK), jnp.tile(gnorm, B_HEADS).reshape(1, B_OUT))
    return o.reshape(b * s, B_OUT)


def token_mixers(pgs, pb, pc, bg, bias, w_gla_gate, b_gla_gate, gla_norm, attn_sinks, b, s):
    outs, lses = [], []
    for gi, (_, dilation) in enumerate(DIL_GROUPS):
        o, lse = attn_dilated_group(pgs[gi], bias, gi, dilation, b, s)
        outs.append(o)
        lses.append(lse)
    o_a = lse_mix(outs, lses)
    o_b = gla(pb, bg, w_gla_gate, b_gla_gate, gla_norm, b, s)
    o_c = attn_swa(pc, bias, attn_sinks, b, s)
    return o_a, o_b, o_c


def _cast_kernel(w_ref, o_ref):
    o_ref[...] = w_ref[...].astype(o_ref.dtype)


def cast_layer(w, l, *, tr=TR_ROWWISE):
    _, rows, cols = w.shape
    return pl.pallas_call(
        _cast_kernel, grid=(rows // tr,),
        in_specs=[pl.BlockSpec((None, tr, cols), lambda i: (l, i, 0))],
        out_specs=pl.BlockSpec((tr, cols), lambda i: (i, 0)),
        out_shape=jax.ShapeDtypeStruct((rows, cols), BF16),
        compiler_params=_cparams(("parallel",)), name="cast_w",
    )(w)


def _cast_halves_kernel(w_ref, lo_ref, hi_ref):
    half = lo_ref.shape[1]
    lo_ref[...] = w_ref[:, :half].astype(lo_ref.dtype)
    hi_ref[...] = w_ref[:, half:].astype(hi_ref.dtype)


def cast_halves(w, l, *, tr=64):
    _, rows, cols = w.shape
    half = cols // 2
    out = pl.BlockSpec((tr, half), lambda i: (i, 0))
    sds = jax.ShapeDtypeStruct((rows, half), BF16)
    return pl.pallas_call(
        _cast_halves_kernel, grid=(rows // tr,),
        in_specs=[pl.BlockSpec((None, tr, cols), lambda i: (l, i, 0))],
        out_specs=[out, out], out_shape=[sds, sds],
        compiler_params=_cparams(("parallel",)), name="cast_halves",
    )(w)


A_SEC = A_HEADS * HEAD_DIM


W_IN_STRIP = 128


def _cast_w_in_kernel(w_ref, a0_ref, a1_ref, a2_ref, b_ref, bg_ref, c_ref, g_ref):
    cvt = lambda lo, n: w_ref[lo:lo + n, :].astype(BF16)
    for gi, a_ref in enumerate((a0_ref, a1_ref, a2_ref)):
        for sec in range(3):
            a_ref[sec * STEP_COLS:(sec + 1) * STEP_COLS, :] = cvt(sec * A_SEC + gi * STEP_COLS, STEP_COLS)
    b_ref[...] = cvt(OFF_B, B_COLS)
    bg_ref[...] = cvt(OFF_BG, B_GATE_RANK)
    c_ref[...] = cvt(OFF_C, C_COLS)
    g_ref[...] = cvt(OFF_G, G_COLS)


def cast_w_in(w_in_t, l):
    _, n_in, d = w_in_t.shape
    heights = (3 * STEP_COLS,) * len(DIL_GROUPS) + (B_COLS, B_GATE_RANK, C_COLS, G_COLS)
    return pl.pallas_call(
        _cast_w_in_kernel, grid=(d // W_IN_STRIP,),
        in_specs=[pl.BlockSpec((None, n_in, W_IN_STRIP), lambda j: (l, 0, j))],
        out_specs=[pl.BlockSpec((n, W_IN_STRIP), lambda j: (0, j)) for n in heights],
        out_shape=[jax.ShapeDtypeStruct((n, d), BF16) for n in heights],
        compiler_params=_cparams(("parallel",)), name="cast_w_in",
    )(w_in_t)


def kernel(x, rel_bias, w_in, w_gla_gate, b_gla_gate, gla_norm, attn_sinks, w_br_a, w_br_b, w_br_c, w_out,
           g_pre_mix, g_post_mix, g_pre_ffn, g_post_ffn, w_up, conv_w, conv_b, w_down):
    b, s, d = x.shape
    depth = w_in.shape[0]
    xf = x.reshape(b * s, d)
    xn = rms_cast(xf, g_pre_mix[0])
    bias = bias_tiles(rel_bias)
    w_in_t = jnp.swapaxes(w_in, 1, 2)
    for l in range(depth):
        *w_a, w_b, w_bg, w_c, w_g = cast_w_in(w_in_t, l)
        proj = functools.partial(matmul, xn, trans_b=True)
        pgs = [proj(w_a[gi], tn=TN_3, out_dtype=BF16 if dil == 1 else F32, name=f"proj_a{gi}")
               for gi, (_, dil) in enumerate(DIL_GROUPS)]
        pb = proj(w_b, name="proj_b")
        pc = proj(w_c, tn=TN_3, name="proj_c")
        gates = proj(w_g, act="sigmoid", name="proj_g")
        bg = proj(w_bg, out_dtype=F32, name="proj_bg")

        o_a, o_b, o_c = token_mixers(pgs, pb, pc, bg, bias, w_gla_gate[l], b_gla_gate[l], gla_norm[l],
                                     attn_sinks[l], b, s)

        merged = merge(o_a, o_b, o_c, gates, cast_layer(w_br_a, l), cast_layer(w_br_b, l), cast_layer(w_br_c, l))
        y = matmul(merged, cast_layer(w_out, l), name="proj_out")
        xf, hn = norm_residual(y, xf, g_post_mix[l], g_pre_ffn[l])

        w_gate, w_upper = cast_halves(w_up, l)
        h = ffn_up(hn, w_gate, w_upper, conv_w[l][:, :D_FF], conv_w[l][:, D_FF:],
                   conv_b[l][:D_FF].reshape(1, D_FF), conv_b[l][D_FF:].reshape(1, D_FF), seq=s)
        f = matmul(h, cast_layer(w_down, l), w_outer=True, tm=T_DOWN, tn=T_DOWN, name="ffn_down")
        xf, xn = norm_residual(f, xf, g_post_ffn[l], g_pre_mix[l + 1] if l + 1 < depth else None)
    return xf.reshape(b, s, d)
```
